```python
import jax, jax.numpy as jnp
from jax import lax
import numpy as np

D_MODEL = 1024
BATCH = 16
SEQ = 2048
DEPTH = 2

N_MIXERS = 2
N_CONV = (DEPTH + 1) // 2
N_NSA = DEPTH // 2

CONV_WIDTH = 31

HEAD_DIM = 64
N_HEADS = D_MODEL // HEAD_DIM
N_KV = 4
GROUP = N_HEADS // N_KV
CMP_BLOCK = 32
CMP_STRIDE = 16
CMP_HIDDEN = 256
SLC_BLOCK = 64
N_SEL = 16
WINDOW = 512
Q_BLOCK = 32
ROPE_THETA = 10000.0
NSA_IN = N_HEADS * HEAD_DIM + 6 * N_KV * HEAD_DIM + 3 * N_HEADS

N_GROUPS = 4
EXPERTS_PER_GROUP = 8
N_EXPERTS = N_GROUPS * EXPERTS_PER_GROUP
EXPERT_TOPK = 2
D_EXPERT = 512
EXPERT_BLOCK = 256

PLE_DIM = 256

EPS = 1e-6
NEG = -1e30

kernel_name = "hybrid_conv_nsa_hmoe_ple"


def rms_norm(x, g):
    xf = x.astype(jnp.float32)
    y = xf * lax.rsqrt(jnp.mean(xf * xf, axis=-1, keepdims=True) + EPS)
    return (y * g.astype(jnp.float32)).astype(x.dtype)


def layer_norm(x, g, b):
    xf = x.astype(jnp.float32)
    mu = jnp.mean(xf, axis=-1, keepdims=True)
    var = jnp.mean(jnp.square(xf - mu), axis=-1, keepdims=True)
    y = (xf - mu) * lax.rsqrt(var + EPS)
    return (y * g.astype(jnp.float32) + b.astype(jnp.float32)).astype(x.dtype)


def rope(x, pos):
    half = x.shape[-1] // 2
    inv = 1.0 / (ROPE_THETA ** (jnp.arange(half, dtype=jnp.float32) / half))
    ang = pos.astype(jnp.float32)[..., None] * inv
    cos = jnp.cos(ang)[:, :, None, :]
    sin = jnp.sin(ang)[:, :, None, :]
    xf = x.astype(jnp.float32)
    x1, x2 = xf[..., :half], xf[..., half:]
    return jnp.concatenate([x1 * cos - x2 * sin, x2 * cos + x1 * sin], axis=-1).astype(x.dtype)


def masked_softmax(s, mask):
    p = jax.nn.softmax(jnp.where(mask, s, NEG), axis=-1)
    return p * mask


def conv_module(xn, w_pw1, b_pw1, w_dw, b_dw, ln_g, ln_b, w_pw2, b_pw2):
    u = xn @ w_pw1 + b_pw1
    a, g = jnp.split(u, 2, axis=-1)
    u = a * jax.nn.sigmoid(g)
    u = lax.conv_general_dilated(
        u, w_dw[:, None, :].astype(u.dtype), window_strides=(1,),
        padding=[(CONV_WIDTH - 1, 0)], dimension_numbers=("NWC", "WIO", "NWC"),
        feature_group_count=u.shape[-1]) + b_dw
    u = jax.nn.silu(layer_norm(u, ln_g, ln_b))
    return u @ w_pw2 + b_pw2


def compress(x, pe, w1, w2):
    B, T, G, dh = x.shape
    xr = x.reshape(B, T // CMP_STRIDE, CMP_STRIDE, G, dh)
    blocks = jnp.concatenate([xr[:, :-1], xr[:, 1:]], axis=2)
    blocks = blocks + pe[None, None, :, None, :]
    blocks = jnp.transpose(blocks, (0, 1, 3, 2, 4)).reshape(B, -1, G, CMP_BLOCK * dh)
    return jax.nn.gelu(blocks @ w1) @ w2


def nsa_attention(xn, positions, w_in, q_norm, k_norm, cmp_pe, cmp_w1, cmp_w2, w_out):
    B, T, _ = xn.shape
    H, G, R, dh = N_HEADS, N_KV, GROUP, HEAD_DIM
    HD, KD = H * dh, G * dh
    NC = T // CMP_STRIDE - 1
    NS = T // SLC_BLOCK
    NQ = T // Q_BLOCK
    n_sel = min(N_SEL, NS)
    a = SLC_BLOCK // CMP_STRIDE
    b = CMP_BLOCK // CMP_STRIDE
    scale = HEAD_DIM ** -0.5

    proj = xn @ w_in
    q = proj[..., :HD].reshape(B, T, H, dh)
    kv = proj[..., HD:HD + 6 * KD].reshape(B, T, 6, G, dh)
    gate = jax.nn.sigmoid(proj[..., HD + 6 * KD:].astype(jnp.float32)).reshape(B, T, G, R, 3)

    q = rope(rms_norm(q, q_norm), positions)
    k_s = rope(rms_norm(kv[:, :, 2], k_norm[1]), positions)
    k_w = rope(rms_norm(kv[:, :, 4], k_norm[2]), positions)
    kc = compress(kv[:, :, 0], cmp_pe[0], cmp_w1[0], cmp_w2[0])
    kc = rope(rms_norm(kc, k_norm[0]), positions[:, CMP_BLOCK - 1::CMP_STRIDE])
    vc = compress(kv[:, :, 1], cmp_pe[1], cmp_w1[1], cmp_w2[1])

    to_g = lambda t: jnp.moveaxis(t, 2, 1)
    kc, vc = to_g(kc), to_g(vc)
    ks_blk = to_g(k_s).reshape(B, G, NS, SLC_BLOCK, dh)
    vs_blk = to_g(kv[:, :, 3]).reshape(B, G, NS, SLC_BLOCK, dh)
    pad_w = ((0, 0), (0, 0), (WINDOW, 0), (0, 0))
    kw_pad = jnp.pad(to_g(k_w), pad_w)
    vw_pad = jnp.pad(to_g(kv[:, :, 5]), pad_w)

    q_chunks = jnp.moveaxis(jnp.transpose(q.reshape(B, T, G, R, dh), (0, 2, 3, 1, 4))
                            .reshape(B, G, R, NQ, Q_BLOCK, dh), 3, 0)
    g_chunks = jnp.moveaxis(jnp.transpose(gate, (0, 2, 3, 1, 4))
                            .reshape(B, G, R, NQ, Q_BLOCK, 3), 3, 0)

    cmp_end = CMP_STRIDE * jnp.arange(NC) + CMP_BLOCK - 1
    blk = jnp.arange(NS)
    bi = jnp.arange(B)[:, None, None, None]
    gi = jnp.arange(G)[None, :, None, None]

    def block_fn(args):
        qc, gc, c = args
        t0 = c * Q_BLOCK
        tq = t0 + jnp.arange(Q_BLOCK)
        s = jnp.einsum('bgrqd,bgkd->bgrqk', qc, kc, preferred_element_type=jnp.float32) * scale
        p_c = masked_softmax(s, cmp_end[None, :] <= tq[:, None])
        o_c = jnp.einsum('bgrqk,bgkd->bgrqd', p_c.astype(vc.dtype), vc)
        imp = p_c.sum(axis=2)
        imp = jnp.pad(imp, ((0, 0), (0, 0), (0, 0), (0, a * NS + a + b - NC)))
        p_slc = sum(imp[..., m + n: m + n + a * NS: a] for m in range(a) for n in range(b))
        cur = tq // SLC_BLOCK
        forced = (blk[None, :] == 0) | (blk[None, :] == cur[:, None]) | (blk[None, :] == cur[:, None] - 1)
        valid = blk[None, :] * SLC_BLOCK <= tq[:, None]
        score = jnp.where(forced, jnp.inf, jnp.where(valid, p_slc, -jnp.inf))
        _, idx = lax.top_k(score, n_sel)
        k_sel = ks_blk[bi, gi, idx].reshape(B, G, Q_BLOCK, n_sel * SLC_BLOCK, dh)
        v_sel = vs_blk[bi, gi, idx].reshape(B, G, Q_BLOCK, n_sel * SLC_BLOCK, dh)
        kpos = (idx[..., None] * SLC_BLOCK + jnp.arange(SLC_BLOCK)).reshape(B, G, Q_BLOCK, -1)
        mask_s = (kpos <= tq[None, None, :, None])[:, :, None]
        s = jnp.einsum('bgrqd,bgqkd->bgrqk', qc, k_sel, preferred_element_type=jnp.float32) * scale
        p_s = masked_softmax(s, mask_s)
        o_s = jnp.einsum('bgrqk,bgqkd->bgrqd', p_s.astype(v_sel.dtype), v_sel)
        kw = lax.dynamic_slice_in_dim(kw_pad, t0, WINDOW + Q_BLOCK, axis=2)
        vw = lax.dynamic_slice_in_dim(vw_pad, t0, WINDOW + Q_BLOCK, axis=2)
        kpos_w = t0 - WINDOW + jnp.arange(WINDOW + Q_BLOCK)
        mask_w = ((kpos_w[None, :] <= tq[:, None]) & (kpos_w[None, :] > tq[:, None] - WINDOW)
                  & (kpos_w[None, :] >= 0))
        s = jnp.einsum('bgrqd,bgkd->bgrqk', qc, kw, preferred_element_type=jnp.float32) * scale
        p_w = masked_softmax(s, mask_w)
        o_w = jnp.einsum('bgrqk,bgkd->bgrqd', p_w.astype(vw.dtype), vw)
        o = gc[..., 0:1] * o_c + gc[..., 1:2] * o_s + gc[..., 2:3] * o_w
        return o.astype(qc.dtype)

    out = lax.map(block_fn, (q_chunks, g_chunks, jnp.arange(NQ)))
    out = jnp.transpose(out, (1, 0, 4, 2, 3, 5)).reshape(B, T, HD)
    return out @ w_out


def hier_moe(xn, w_rg, b_rg, w_re, b_re, w_gate, w_up, w_down):
    B, T, D = xn.shape
    N = B * T
    xf = xn.reshape(N, D)
    lg = (xf @ w_rg + b_rg).astype(jnp.float32)
    pg = jax.nn.softmax(lg, axis=-1)
    pg_sel, grp = lax.top_k(pg, 1)
    le = (xf @ w_re + b_re).astype(jnp.float32).reshape(N, N_GROUPS, EXPERTS_PER_GROUP)
    le_g = jnp.take_along_axis(le, grp[:, :, None], axis=1)[:, 0]
    top_v, top_i = lax.top_k(le_g, EXPERT_TOPK)
    wts = jax.nn.softmax(top_v, axis=-1) * pg_sel
    eid = grp * EXPERTS_PER_GROUP + top_i
    A = N * EXPERT_TOPK
    e_flat = eid.reshape(A)
    tok = jnp.repeat(jnp.arange(N), EXPERT_TOPK)
    w_flat = wts.reshape(A)
    order = jnp.argsort(e_flat)
    e_s = e_flat[order]
    counts = jnp.zeros((N_EXPERTS,), jnp.int32).at[e_flat].add(1)
    pc = (counts + EXPERT_BLOCK - 1) // EXPERT_BLOCK * EXPERT_BLOCK
    ends = jnp.cumsum(pc)
    starts = jnp.cumsum(counts) - counts
    dest = (ends - pc)[e_s] + jnp.arange(A) - starts[e_s]
    n_blocks = -(-A // EXPERT_BLOCK) + N_EXPERTS
    P = n_blocks * EXPERT_BLOCK
    slot_tok = jnp.zeros((P,), jnp.int32).at[dest].set(tok[order])
    slot_w = jnp.zeros((P,), jnp.float32).at[dest].set(w_flat[order])
    blk_e = jnp.minimum(jnp.searchsorted(ends, jnp.arange(n_blocks) * EXPERT_BLOCK, side='right'),
                        N_EXPERTS - 1)
    xb = xf[slot_tok].reshape(n_blocks, EXPERT_BLOCK, D)

    def expert_block(args):
        xblk, e = args
        h = jax.nn.silu(xblk @ w_gate[e]) * (xblk @ w_up[e])
        return h @ w_down[e]

    yb = lax.map(expert_block, (xb, blk_e)).reshape(P, D)
    out = jnp.zeros((N, D), yb.dtype).at[slot_tok].add(yb * slot_w[:, None].astype(yb.dtype))
    return out.reshape(B, T, D)


def setup_inputs(seed: int = 0) -> dict:
    key = jax.random.key(seed)
    ks = iter(jax.random.split(key, 40))
    D, F = D_MODEL, D_EXPERT

    def nrm(shape, fan_in):
        return jax.random.normal(next(ks), shape, jnp.float32) * (fan_in ** -0.5)

    def gain(shape):
        return 1.0 + 0.02 * jax.random.normal(next(ks), shape, jnp.float32)

    def small(shape, s=0.02):
        return s * jax.random.normal(next(ks), shape, jnp.float32)

    x = jax.random.normal(next(ks), (BATCH, SEQ, D), jnp.float32)
    p = jax.random.normal(next(ks), (DEPTH, BATCH, SEQ, PLE_DIM), jnp.float32)
    offset = jax.random.randint(next(ks), (BATCH, 1), 0, 1024, dtype=jnp.int32)
    positions = (offset + jnp.arange(SEQ, dtype=jnp.int32)[None, :]).astype(jnp.int32)
    return {
        "x": x, "p": p, "positions": positions,
        "g_mix": gain((DEPTH, D)), "g_ffn": gain((DEPTH, D)), "g_ple": gain((DEPTH, D)),
        "conv_w_pw1": nrm((N_CONV, D, 2 * D), D), "conv_b_pw1": small((N_CONV, 2 * D)),
        "conv_w_dw": nrm((N_CONV, CONV_WIDTH, D), CONV_WIDTH), "conv_b_dw": small((N_CONV, D)),
        "conv_ln_g": gain((N_CONV, D)), "conv_ln_b": small((N_CONV, D)),
        "conv_w_pw2": nrm((N_CONV, D, D), D), "conv_b_pw2": small((N_CONV, D)),
        "nsa_w_in": nrm((N_NSA, D, NSA_IN), D),
        "nsa_q_norm": gain((N_NSA, HEAD_DIM)), "nsa_k_norm": gain((N_NSA, 3, HEAD_DIM)),
        "nsa_cmp_pe": small((N_NSA, 2, CMP_BLOCK, HEAD_DIM), 0.1),
        "nsa_cmp_w1": nrm((N_NSA, 2, CMP_BLOCK * HEAD_DIM, CMP_HIDDEN), CMP_BLOCK * HEAD_DIM),
        "nsa_cmp_w2": nrm((N_NSA, 2, CMP_HIDDEN, HEAD_DIM), CMP_HIDDEN),
        "nsa_w_out": nrm((N_NSA, N_HEADS * HEAD_DIM, D), N_HEADS * HEAD_DIM),
        "moe_w_rg": nrm((DEPTH, D, N_GROUPS), D), "moe_b_rg": small((DEPTH, N_GROUPS), 0.01),
        "moe_w_re": nrm((DEPTH, D, N_EXPERTS), D), "moe_b_re": small((DEPTH, N_EXPERTS), 0.01),
        "moe_w_gate": nrm((DEPTH, N_EXPERTS, D, F), D), "moe_w_up": nrm((DEPTH, N_EXPERTS, D, F), D),
        "moe_w_down": nrm((DEPTH, N_EXPERTS, F, D), F),
        "ple_w_proj": nrm((DEPTH, PLE_DIM, D), PLE_DIM), "ple_w_gate": nrm((DEPTH, D, D), D),
    }


def reference(x, p, positions, g_mix, g_ffn, g_ple, conv_w_pw1, conv_b_pw1, conv_w_dw, conv_b_dw,
              conv_ln_g, conv_ln_b, conv_w_pw2, conv_b_pw2, nsa_w_in, nsa_q_norm, nsa_k_norm,
              nsa_cmp_pe, nsa_cmp_w1, nsa_cmp_w2, nsa_w_out, moe_w_rg, moe_b_rg, moe_w_re, moe_b_re,
              moe_w_gate, moe_w_up, moe_w_down, ple_w_proj, ple_w_gate):
    h = x
    for i in range(DEPTH):
        xn = rms_norm(h, g_mix[i])
        j = i // N_MIXERS
        if i % N_MIXERS == 0:
            h = h + conv_module(xn, conv_w_pw1[j], conv_b_pw1[j], conv_w_dw[j], conv_b_dw[j],
                                conv_ln_g[j], conv_ln_b[j], conv_w_pw2[j], conv_b_pw2[j])
        else:
            h = h + nsa_attention(xn, positions, nsa_w_in[j], nsa_q_norm[j], nsa_k_norm[j],
                                  nsa_cmp_pe[j], nsa_cmp_w1[j], nsa_cmp_w2[j], nsa_w_out[j])
        h = h + hier_moe(rms_norm(h, g_ffn[i]), moe_w_rg[i], moe_b_rg[i], moe_w_re[i], moe_b_re[i],
                         moe_w_gate[i], moe_w_up[i], moe_w_down[i])
        gate = jax.nn.sigmoid(rms_norm(h, g_ple[i]) @ ple_w_gate[i])
        h = h + gate * (p[i] @ ple_w_proj[i])
    return h
```

```python
import functools

import jax
import jax.numpy as jnp
from jax import lax
from jax.experimental import pallas as pl
from jax.experimental.pallas import tpu as pltpu

F32 = jnp.float32
BF16 = jnp.bfloat16
I32 = jnp.int32

LANES = 128
VMEM_LIMIT = 56 * 1024 * 1024

EPS = 1e-6
NEG = -1e30

CONV_WIDTH = 31
HEAD_DIM = 64
N_KV = 4
CMP_BLOCK = 32
CMP_STRIDE = 16
SLC_BLOCK = 64
N_SEL = 16
WINDOW = 512
ROPE_THETA = 10000.0
N_GROUPS = 4
EXPERTS_PER_GROUP = 8
N_EXPERTS = N_GROUPS * EXPERTS_PER_GROUP

ROW_TILE = 512
CONV_CHUNK = 32
CONV_HALO = 32
EXPERT_ROWS = 256
MOVE_TILE = 256
Q_TILE = 128
K_TILE = 256


def _params(n_grid):
    return pltpu.CompilerParams(dimension_semantics=("arbitrary",) * n_grid,
                                vmem_limit_bytes=VMEM_LIMIT)


def _dot(a, b):
    return jnp.dot(a, b, preferred_element_type=F32)


def _dot_nt(a, b):
    return lax.dot_general(a, b, (((1,), (1,)), ((), ())), preferred_element_type=F32)


def _rms(x, g):
    return x * lax.rsqrt(jnp.mean(x * x, axis=-1, keepdims=True) + EPS) * g


def _full(shape):
    n = len(shape)
    return pl.BlockSpec(shape, lambda *_: (0,) * n)


def _route(h, g_ffn, wcat_ref, br_ref, tri_ref, carry_ref):
    xn = _rms(h, g_ffn)
    hi = xn.astype(BF16)
    lo = (xn - hi.astype(F32)).astype(BF16)
    r1 = _dot(hi, wcat_ref[...])
    r2 = _dot(lo, wcat_ref[:, :LANES])
    logits = r1[:, :LANES] + r1[:, LANES:] + r2 + br_ref[...]

    lane = lax.broadcasted_iota(I32, logits.shape, 1).astype(F32)
    lg = jnp.where(lane < N_GROUPS, logits, -jnp.inf)
    m = jnp.max(lg, axis=-1, keepdims=True)
    grp = jnp.min(jnp.where(lg == m, lane, float(LANES)), axis=-1, keepdims=True)
    pg_sel = 1.0 / jnp.sum(jnp.exp(lg - m), axis=-1, keepdims=True)

    lo_e = N_GROUPS + EXPERTS_PER_GROUP * grp
    le = jnp.where((lane >= lo_e) & (lane < lo_e + EXPERTS_PER_GROUP), logits, -jnp.inf)
    v1 = jnp.max(le, axis=-1, keepdims=True)
    i1 = jnp.min(jnp.where(le == v1, lane, float(LANES)), axis=-1, keepdims=True)
    le2 = jnp.where(lane == i1, -jnp.inf, le)
    v2 = jnp.max(le2, axis=-1, keepdims=True)
    i2 = jnp.min(jnp.where(le2 == v2, lane, float(LANES)), axis=-1, keepdims=True)
    e21 = jnp.exp(v2 - v1)
    w1 = pg_sel / (1.0 + e21)
    w2 = pg_sel * e21 / (1.0 + e21)
    e1 = i1 - N_GROUPS
    e2 = i2 - N_GROUPS

    oh1 = jnp.where(lane == e1, 1.0, 0.0)
    oh2 = jnp.where(lane == e2, 1.0, 0.0)
    c1 = _dot(tri_ref[...], oh1.astype(BF16))
    c2 = _dot(tri_ref[...], oh2.astype(BF16))
    carry = carry_ref[...]
    tot1 = jnp.sum(oh1, axis=0, keepdims=True)
    tot2 = jnp.sum(oh2, axis=0, keepdims=True)
    rank1 = jnp.sum(oh1 * (c1 + carry), axis=-1, keepdims=True)
    rank2 = jnp.sum(oh2 * (c2 + carry + tot1), axis=-1, keepdims=True)
    carry_ref[...] = carry + tot1 + tot2

    slab = jnp.where(lane == 0, e1, jnp.where(lane == 1, e2, jnp.where(lane == 2, w1, jnp.where(
        lane == 3, w2, jnp.where(lane == 4, rank1, jnp.where(lane == 5, rank2, 0.0))))))
    return xn, slab


def _router_operands(w_rg, b_rg, w_re, b_re, tm):
    d = w_rg.shape[0]
    w = jnp.zeros((d, LANES), F32).at[:, :N_GROUPS].set(w_rg).at[:, N_GROUPS:N_GROUPS + N_EXPERTS].set(w_re)
    w_hi = w.astype(BF16)
    w_lo = (w - w_hi.astype(F32)).astype(BF16)
    wcat = jnp.concatenate([w_hi, w_lo], axis=1)
    br = jnp.zeros((1, LANES), F32).at[0, :N_GROUPS].set(b_rg).at[0, N_GROUPS:N_GROUPS + N_EXPERTS].set(b_re)
    r = lax.broadcasted_iota(I32, (tm, tm), 0)
    c = lax.broadcasted_iota(I32, (tm, tm), 1)
    tri = (c < r).astype(BF16)
    return wcat, br, tri


def _pw1_glu_kernel(x_ref, g_ref, w_ref, b_ref, o_ref, *, chunk):
    xn = _rms(x_ref[...], g_ref[...]).astype(BF16)
    d = o_ref.shape[1]
    for j in range(d // chunk):
        sa = slice(j * chunk, (j + 1) * chunk)
        sg = slice(d + j * chunk, d + (j + 1) * chunk)
        a = _dot(xn, w_ref[:, sa]) + b_ref[:, sa]
        g = _dot(xn, w_ref[:, sg]) + b_ref[:, sg]
        o_ref[:, sa] = a * jax.nn.sigmoid(g)


def _pw1_glu(x2, g, w, b):
    n, d = x2.shape
    tm = ROW_TILE
    return pl.pallas_call(
        functools.partial(_pw1_glu_kernel, chunk=512),
        grid=(n // tm,),
        in_specs=[pl.BlockSpec((tm, d), lambda i: (i, 0)), _full((1, d)), _full((d, 2 * d)), _full((1, 2 * d))],
        out_specs=pl.BlockSpec((tm, d), lambda i: (i, 0)),
        out_shape=jax.ShapeDtypeStruct((n, d), F32),
        compiler_params=_params(1),
        name="pw1_glu",
    )(x2, g.reshape(1, d), w.astype(BF16), b.reshape(1, 2 * d))


def _conv_mix_kernel(u_ref, halo_ref, x_ref, wdw_ref, bdw_ref, lng_ref, lnb_ref, w2_ref, b2_ref,
                     gffn_ref, wcat_ref, br_ref, tri_ref,
                     h_ref, xn_ref, slab_ref, cnt_ref,
                     ext_ref, act_ref, carry_ref):
    b = pl.program_id(0)
    i = pl.program_id(1)
    tt = u_ref.shape[1]

    @pl.when((b == 0) & (i == 0))
    def _():
        carry_ref[...] = jnp.zeros_like(carry_ref)

    halo = halo_ref[0]
    ext_ref[0:CONV_HALO, :] = jnp.where(i == 0, jnp.zeros_like(halo), halo)
    ext_ref[CONV_HALO:, :] = u_ref[0]
    lead = CONV_HALO - (CONV_WIDTH - 1)

    def chunk(c, carry):
        base = pl.multiple_of(c * CONV_CHUNK, CONV_CHUNK)
        win = ext_ref[pl.ds(base, CONV_CHUNK + CONV_HALO), :]
        acc = jnp.zeros((CONV_CHUNK, win.shape[1]), F32)
        for k in range(CONV_WIDTH):
            acc = acc + wdw_ref[k:k + 1, :] * win[lead + k:lead + k + CONV_CHUNK, :]
        acc = acc + bdw_ref[...]
        mu = jnp.mean(acc, axis=-1, keepdims=True)
        cen = acc - mu
        var = jnp.mean(cen * cen, axis=-1, keepdims=True)
        y = cen * lax.rsqrt(var + EPS) * lng_ref[...] + lnb_ref[...]
        act_ref[pl.ds(base, CONV_CHUNK), :] = (y * jax.nn.sigmoid(y)).astype(BF16)
        return carry

    lax.fori_loop(0, tt // CONV_CHUNK, chunk, 0)

    h = x_ref[0] + _dot(act_ref[...], w2_ref[...]) + b2_ref[...]
    h_ref[0] = h
    xn, slab = _route(h, gffn_ref[...], wcat_ref, br_ref, tri_ref, carry_ref)
    xn_ref[...] = xn
    slab_ref[...] = slab
    cnt_ref[...] = carry_ref[...]


def _conv_mix(u, x, w_dw, b_dw, ln_g, ln_b, w_pw2, b_pw2, g_ffn, w_rg, b_rg, w_re, b_re):
    bsz, t, d = x.shape
    tt = ROW_TILE
    nt = t // tt
    n = bsz * t
    wcat, br, tri = _router_operands(w_rg, b_rg, w_re, b_re, tt)
    wdw = jnp.zeros((CONV_HALO, d), F32).at[:CONV_WIDTH].set(w_dw)
    hb = tt // CONV_HALO
    row = lambda v: v.reshape(1, d)
    return pl.pallas_call(
        _conv_mix_kernel,
        grid=(bsz, nt),
        in_specs=[
            pl.BlockSpec((1, tt, d), lambda b, i: (b, i, 0)),
            pl.BlockSpec((1, CONV_HALO, d), lambda b, i: (b, jnp.maximum(i * hb - 1, 0), 0)),
            pl.BlockSpec((1, tt, d), lambda b, i: (b, i, 0)),
            _full((CONV_HALO, d)), _full((1, d)), _full((1, d)), _full((1, d)),
            _full((d, d)), _full((1, d)), _full((1, d)),
            _full((d, 2 * LANES)), _full((1, LANES)), _full((tt, tt)),
        ],
        out_specs=[
            pl.BlockSpec((1, tt, d), lambda b, i: (b, i, 0)),
            pl.BlockSpec((tt, d), lambda b, i: (b * nt + i, 0)),
            pl.BlockSpec((tt, LANES), lambda b, i: (b * nt + i, 0)),
            _full((1, LANES)),
        ],
        out_shape=[
            jax.ShapeDtypeStruct((bsz, t, d), F32),
            jax.ShapeDtypeStruct((n, d), F32),
            jax.ShapeDtypeStruct((n, LANES), F32),
            jax.ShapeDtypeStruct((1, LANES), F32),
        ],
        scratch_shapes=[pltpu.VMEM((CONV_HALO + tt, d), F32), pltpu.VMEM((tt, d), BF16),
                        pltpu.VMEM((1, LANES), F32)],
        compiler_params=_params(2),
        name="conv_mix",
    )(u, u, x, wdw, row(b_dw), row(ln_g), row(ln_b), w_pw2.astype(BF16), row(b_pw2), row(g_ffn), wcat, br, tri)


def _dispatch_kernel(dest_ref, xn_ref, xs_in_ref, xs_ref, sem):
    del xs_in_ref
    tm = xn_ref.shape[0]
    base = pl.program_id(0) * (2 * tm)

    def row_copy(r, d):
        return pltpu.make_async_copy(xn_ref.at[pl.ds(r, 1)], xs_ref.at[pl.ds(d, 1)], sem)

    def issue(r, carry):
        row_copy(r, dest_ref[base + 2 * r]).start()
        row_copy(r, dest_ref[base + 2 * r + 1]).start()
        return carry

    lax.fori_loop(0, tm, issue, 0)

    def drain(r, carry):
        row_copy(r, 0).wait()
        row_copy(r, 0).wait()
        return carry

    lax.fori_loop(0, tm, drain, 0)


def _dispatch(dest_flat, xn, n_slots):
    n, d = xn.shape
    tm = MOVE_TILE
    return pl.pallas_call(
        _dispatch_kernel,
        grid_spec=pltpu.PrefetchScalarGridSpec(
            num_scalar_prefetch=1,
            grid=(n // tm,),
            in_specs=[pl.BlockSpec((tm, d), lambda i, dest: (i, 0)), pl.BlockSpec(memory_space=pl.ANY)],
            out_specs=pl.BlockSpec(memory_space=pl.ANY),
            scratch_shapes=[pltpu.SemaphoreType.DMA(())],
        ),
        out_shape=jax.ShapeDtypeStruct((n_slots, d), F32),
        input_output_aliases={2: 0},
        compiler_params=_params(1),
        name="moe_dispatch",
    )(dest_flat, xn, jnp.zeros((n_slots, d), F32))


def _expert_kernel(be_ref, nu_ref, xs_ref, wg_ref, wu_ref, wd_ref, y_ref, wgb_ref, wub_ref, wdb_ref):
    i = pl.program_id(0)
    changed = (i == 0) | (be_ref[i] != be_ref[jnp.maximum(i - 1, 0)])

    @pl.when(changed)
    def _():
        wgb_ref[...] = wg_ref[0].astype(BF16)
        wub_ref[...] = wu_ref[0].astype(BF16)
        wdb_ref[...] = wd_ref[0].astype(BF16)

    @pl.when(i < nu_ref[0])
    def _():
        x = xs_ref[...].astype(BF16)
        a = _dot(x, wgb_ref[...])
        u = _dot(x, wub_ref[...])
        hid = (a * jax.nn.sigmoid(a) * u).astype(BF16)
        y_ref[...] = _dot(hid, wdb_ref[...])

    @pl.when(i >= nu_ref[0])
    def _():
        y_ref[...] = jnp.zeros_like(y_ref)


def _experts(blk_e, n_used, xs, w_gate, w_up, w_down):
    p_rows, d = xs.shape
    f = w_gate.shape[2]
    be = EXPERT_ROWS
    return pl.pallas_call(
        _expert_kernel,
        grid_spec=pltpu.PrefetchScalarGridSpec(
            num_scalar_prefetch=2,
            grid=(p_rows // be,),
            in_specs=[
                pl.BlockSpec((be, d), lambda i, e, nu: (i, 0)),
                pl.BlockSpec((1, d, f), lambda i, e, nu: (e[i], 0, 0)),
                pl.BlockSpec((1, d, f), lambda i, e, nu: (e[i], 0, 0)),
                pl.BlockSpec((1, f, d), lambda i, e, nu: (e[i], 0, 0)),
            ],
            out_specs=pl.BlockSpec((be, d), lambda i, e, nu: (i, 0)),
            scratch_shapes=[pltpu.VMEM((d, f), BF16), pltpu.VMEM((d, f), BF16), pltpu.VMEM((f, d), BF16)],
        ),
        out_shape=jax.ShapeDtypeStruct((p_rows, d), F32),
        compiler_params=_params(1),
        name="moe_experts",
    )(blk_e, n_used, xs, w_gate, w_up, w_down)


def _combine_ple_kernel(dest_ref, h_ref, slab_ref, p_ref, gple_ref, wg_ref, wp_ref, y_hbm,
                        o_ref, ybuf, sems):
    tm = h_ref.shape[0]
    i = pl.program_id(0)
    n_steps = pl.num_programs(0)
    slot = i % 2

    def row_copy(step_slot, k, r, d):
        return pltpu.make_async_copy(y_hbm.at[pl.ds(d, 1)], ybuf.at[step_slot, k, pl.ds(r, 1)],
                                     sems.at[step_slot])

    def issue(step, step_slot):
        base = step * (2 * tm)

        def body(r, carry):
            row_copy(step_slot, 0, r, dest_ref[base + 2 * r]).start()
            row_copy(step_slot, 1, r, dest_ref[base + 2 * r + 1]).start()
            return carry

        lax.fori_loop(0, tm, body, 0)

    @pl.when(i == 0)
    def _():
        issue(0, 0)

    @pl.when(i + 1 < n_steps)
    def _():
        issue(i + 1, 1 - slot)

    def drain(r, carry):
        row_copy(slot, 0, r, 0).wait()
        row_copy(slot, 1, r, 0).wait()
        return carry

    lax.fori_loop(0, tm, drain, 0)

    slab = slab_ref[...]
    lane = lax.broadcasted_iota(I32, slab.shape, 1)
    w0 = jnp.sum(jnp.where(lane == 2, slab, 0.0), axis=-1, keepdims=True)
    w1 = jnp.sum(jnp.where(lane == 3, slab, 0.0), axis=-1, keepdims=True)
    hm = h_ref[...] + w0 * ybuf[slot, 0] + w1 * ybuf[slot, 1]
    gate = jax.nn.sigmoid(_dot(_rms(hm, gple_ref[...]).astype(BF16), wg_ref[...]))
    o_ref[...] = hm + gate * _dot(p_ref[...].astype(BF16), wp_ref[...])


def _combine_ple(dest_flat, h2, slab, p2, g_ple, w_gate, w_proj, y):
    n, d = h2.shape
    dp = p2.shape[1]
    tm = MOVE_TILE
    return pl.pallas_call(
        _combine_ple_kernel,
        grid_spec=pltpu.PrefetchScalarGridSpec(
            num_scalar_prefetch=1,
            grid=(n // tm,),
            in_specs=[
                pl.BlockSpec((tm, d), lambda i, dest: (i, 0)),
                pl.BlockSpec((tm, LANES), lambda i, dest: (i, 0)),
                pl.BlockSpec((tm, dp), lambda i, dest: (i, 0)),
                pl.BlockSpec((1, d), lambda i, dest: (0, 0)),
                pl.BlockSpec((d, d), lambda i, dest: (0, 0)),
                pl.BlockSpec((dp, d), lambda i, dest: (0, 0)),
                pl.BlockSpec(memory_space=pl.ANY),
            ],
            out_specs=pl.BlockSpec((tm, d), lambda i, dest: (i, 0)),
            scratch_shapes=[pltpu.VMEM((2, 2, tm, d), F32), pltpu.SemaphoreType.DMA((2,))],
        ),
        out_shape=jax.ShapeDtypeStruct((n, d), F32),
        compiler_params=_params(1),
        name="moe_combine_ple",
    )(dest_flat, h2, slab, p2, g_ple.reshape(1, d), w_gate.astype(BF16), w_proj.astype(BF16), y)


def _moe_ple(h2, xn, slab, counts, p2, g_ple, w_gate_e, w_up_e, w_down_e, ple_w_gate, ple_w_proj):
    n, d = h2.shape
    be = EXPERT_ROWS
    n_blocks = (2 * n) // be + N_EXPERTS
    cnt = counts[0, :N_EXPERTS].astype(I32)
    padded = (cnt + be - 1) // be * be
    ends = jnp.cumsum(padded)
    starts = ends - padded
    eid = slab[:, 0:2].astype(I32)
    rank = slab[:, 4:6].astype(I32)
    dest_flat = (starts[eid] + rank).reshape(2 * n)
    blk_e = jnp.minimum(jnp.searchsorted(ends, jnp.arange(n_blocks, dtype=I32) * be, side="right"),
                        N_EXPERTS - 1).astype(I32)
    n_used = (ends[-1:] // be).astype(I32)
    xs = _dispatch(dest_flat, xn, n_blocks * be)
    y = _experts(blk_e, n_used, xs, w_gate_e, w_up_e, w_down_e)
    return _combine_ple(dest_flat, h2, slab, p2, g_ple, ple_w_gate, ple_w_proj, y)


def _rope_table_kernel(pos_ref, inv_ref, sign_ref, cos_ref, sin_ref):
    ang = pos_ref[...] * inv_ref[...]
    cos_ref[...] = jnp.cos(ang)
    sin_ref[...] = jnp.sin(ang) * sign_ref[...]


def _rope_table(positions):
    n = positions.size
    tm = ROW_TILE
    half = HEAD_DIM // 2
    lane = jnp.arange(LANES)
    inv = 1.0 / (ROPE_THETA ** ((lane % half).astype(F32) / half))
    sign = jnp.where(lane % HEAD_DIM < half, -1.0, 1.0).astype(F32)
    return pl.pallas_call(
        _rope_table_kernel,
        grid=(n // tm,),
        in_specs=[pl.BlockSpec((tm, 1), lambda i: (i, 0)), _full((1, LANES)), _full((1, LANES))],
        out_specs=[pl.BlockSpec((tm, LANES), lambda i: (i, 0))] * 2,
        out_shape=[jax.ShapeDtypeStruct((n, LANES), F32)] * 2,
        compiler_params=_params(1),
        name="rope_table",
    )(positions.astype(F32).reshape(n, 1), inv.reshape(1, LANES), sign.reshape(1, LANES))


def _head_norm_rope(z, gain, seg_ref, cos, sin):
    z2 = z * z
    hi = z2.astype(BF16)
    lo = (z2 - hi.astype(F32)).astype(BF16)
    ssq = _dot(hi, seg_ref[...]) + _dot(lo, seg_ref[...])
    zn = z * lax.rsqrt(ssq * (1.0 / HEAD_DIM) + EPS) * gain
    width = z.shape[1]
    half = HEAD_DIM // 2
    lane = lax.broadcasted_iota(I32, z.shape, 1)
    rot = jnp.where((lane & half) == 0, pltpu.roll(zn, width - half, 1), pltpu.roll(zn, half, 1))
    return zn * cos + rot * sin


def _nsa_proj_kernel(h_ref, g_ref, w_ref, cos_ref, sin_ref, qn_ref, kns_ref, knw_ref, seg_ref,
                     q_ref, kc_ref, vc_ref, ks_ref, vs_ref, kw_ref, vw_ref, gate_ref):
    d = h_ref.shape[1]
    kd = N_KV * HEAD_DIM
    xn = _rms(h_ref[...], g_ref[...]).astype(BF16)
    cos = jnp.concatenate([cos_ref[...], cos_ref[...]], axis=1)
    sin = jnp.concatenate([sin_ref[...], sin_ref[...]], axis=1)
    scale = HEAD_DIM ** -0.5

    def heads_out(ref, first, z):
        for r in range(N_KV):
            ref[0, first + r] = z[:, r * HEAD_DIM:(r + 1) * HEAD_DIM].astype(ref.dtype)

    for j in range(d // kd):
        z = _dot(xn, w_ref[:, j * kd:(j + 1) * kd])
        heads_out(q_ref, N_KV * j, _head_norm_rope(z, qn_ref[...], seg_ref, cos, sin) * scale)
    col = lambda c: _dot(xn, w_ref[:, d + c * kd:d + (c + 1) * kd])
    kc_ref[...] = col(0)
    vc_ref[...] = col(1)
    heads_out(ks_ref, 0, _head_norm_rope(col(2), kns_ref[...], seg_ref, cos, sin))
    heads_out(vs_ref, 0, col(3))
    heads_out(kw_ref, 0, _head_norm_rope(col(4), knw_ref[...], seg_ref, cos, sin))
    heads_out(vw_ref, 0, col(5))
    gate_ref[...] = jax.nn.sigmoid(_dot(xn, w_ref[:, d + 6 * kd:]))


def _nsa_proj(h, g_mix, w_in, cos_t, sin_t, q_norm, k_norm):
    bsz, t, d = h.shape
    n = bsz * t
    tm = ROW_TILE
    nt = t // tm
    kd = N_KV * HEAD_DIM
    n_heads = d // HEAD_DIM
    n_in = w_in.shape[1]
    w_pad = jnp.zeros((d, d + 6 * kd + LANES), F32).at[:, :n_in].set(w_in).astype(BF16)
    tile4 = lambda v: jnp.tile(v, kd // HEAD_DIM).reshape(1, kd)
    idx = jnp.arange(kd) // HEAD_DIM
    seg = (idx[:, None] == idx[None, :]).astype(BF16)
    head_spec = lambda nh: pl.BlockSpec((1, nh, tm, HEAD_DIM), lambda i: (i // nt, 0, i % nt, 0))
    head_shape = lambda nh: jax.ShapeDtypeStruct((bsz, nh, t, HEAD_DIM), BF16)
    flat_spec = lambda w: pl.BlockSpec((tm, w), lambda i: (i, 0))
    return pl.pallas_call(
        _nsa_proj_kernel,
        grid=(n // tm,),
        in_specs=[flat_spec(d), _full((1, d)), _full(w_pad.shape), flat_spec(LANES), flat_spec(LANES),
                  _full((1, kd)), _full((1, kd)), _full((1, kd)), _full((kd, kd))],
        out_specs=[head_spec(n_heads), flat_spec(kd), flat_spec(kd), head_spec(N_KV), head_spec(N_KV),
                   head_spec(N_KV), head_spec(N_KV), flat_spec(LANES)],
        out_shape=[head_shape(n_heads), jax.ShapeDtypeStruct((n, kd), F32), jax.ShapeDtypeStruct((n, kd), F32),
                   head_shape(N_KV), head_shape(N_KV), head_shape(N_KV), head_shape(N_KV),
                   jax.ShapeDtypeStruct((n, LANES), F32)],
        compiler_params=_params(1),
        name="nsa_proj",
    )(h.reshape(n, d), g_mix.reshape(1, d), w_pad, cos_t, sin_t,
      tile4(q_norm), tile4(k_norm[1]), tile4(k_norm[2]), seg)


def _compress_kernel(x_ref, pe_ref, w1_ref, w2_ref, kn_ref, cos_ref, sin_ref, o_ref):
    which = pl.program_id(1)
    half_w = x_ref.shape[-1]
    x = x_ref[0, 0, 0]
    first = _dot((x + pe_ref[0, 0:1, :]).astype(BF16), w1_ref[0, :half_w, :])
    second = _dot((x + pe_ref[0, 1:2, :]).astype(BF16), w1_ref[0, half_w:, :])
    n_chunks = x.shape[0]
    hid = first + pltpu.roll(second, n_chunks - 1, 0)
    c = _dot(jax.nn.gelu(hid, approximate=True).astype(BF16), w2_ref[0])
    half = HEAD_DIM // 2
    cn = _rms(c, kn_ref[...])
    rot = jnp.concatenate([cn[:, half:], cn[:, :half]], axis=1)
    ck = cn * cos_ref[0] + rot * sin_ref[0]
    o_ref[0, 0, 0] = jnp.where(which == 0, ck, c).astype(o_ref.dtype)


def _compress(kc_raw, vc_raw, bsz, t, pe, w1, w2, k_norm0, cos_t, sin_t):
    g, dh, st = N_KV, HEAD_DIM, CMP_STRIDE
    nch = t // st
    wide = st * dh
    hidden = w1.shape[-1]

    def chunks(a):
        return a.reshape(bsz, nch, st, g, dh).transpose(0, 3, 1, 2, 4).reshape(bsz, g, nch, wide)

    x = jnp.stack([chunks(kc_raw), chunks(vc_raw)])
    pe2 = pe.reshape(2, 2, wide)
    last = lambda tab: jnp.pad(tab.reshape(bsz, t, LANES)[:, CMP_BLOCK - 1::st, :dh], ((0, 0), (0, 1), (0, 0)))
    return pl.pallas_call(
        _compress_kernel,
        grid=(bsz, 2, g),
        in_specs=[
            pl.BlockSpec((1, 1, 1, nch, wide), lambda b, w, gi: (w, b, gi, 0, 0)),
            pl.BlockSpec((1, 2, wide), lambda b, w, gi: (w, 0, 0)),
            pl.BlockSpec((1, 2 * wide, hidden), lambda b, w, gi: (w, 0, 0)),
            pl.BlockSpec((1, hidden, dh), lambda b, w, gi: (w, 0, 0)),
            pl.BlockSpec((1, dh), lambda b, w, gi: (0, 0)),
            pl.BlockSpec((1, nch, dh), lambda b, w, gi: (b, 0, 0)),
            pl.BlockSpec((1, nch, dh), lambda b, w, gi: (b, 0, 0)),
        ],
        out_specs=pl.BlockSpec((1, 1, 1, nch, dh), lambda b, w, gi: (w, b, gi, 0, 0)),
        out_shape=jax.ShapeDtypeStruct((2, bsz, g, nch, dh), BF16),
        compiler_params=_params(3),
        name="nsa_compress",
    )(x, pe2, w1.astype(BF16), w2.astype(BF16), k_norm0.reshape(1, dh), last(cos_t), last(sin_t))


def _masked_softmax_parts(s3, mask):
    s3 = jnp.where(mask[None], s3, NEG)
    m = jnp.max(s3, axis=-1, keepdims=True)
    e = jnp.where(mask[None], jnp.exp(s3 - m), 0.0)
    l = jnp.sum(e, axis=-1, keepdims=True)
    return e * jnp.where(l > 0.0, 1.0 / l, 0.0)


def _nsa_attn_kernel(q_ref, kc_ref, vc_ref, ks_ref, vs_ref, kw_ref, vw_ref, gate_ref, wsel_ref, o_ref,
                     *, n_sel):
    g = pl.program_id(1)
    qi = pl.program_id(2)
    r_heads, tq_n, dh = q_ref.shape[1], q_ref.shape[2], q_ref.shape[3]
    rows = r_heads * tq_n
    t0 = qi * tq_n
    q = q_ref[0].reshape(rows, dh)

    kc = kc_ref[0, 0]
    nc = kc.shape[0]
    tq_c = t0 + lax.broadcasted_iota(I32, (tq_n, nc), 0)
    cend = CMP_STRIDE * lax.broadcasted_iota(I32, (tq_n, nc), 1) + (CMP_BLOCK - 1)
    p_c = _masked_softmax_parts(_dot_nt(q, kc).reshape(r_heads, tq_n, nc), cend <= tq_c)
    o_c = _dot(p_c.reshape(rows, nc).astype(BF16), vc_ref[0, 0])

    imp = jnp.sum(p_c, axis=0)
    imp_hi = imp.astype(BF16)
    imp_lo = (imp - imp_hi.astype(F32)).astype(BF16)
    p_slc = _dot_nt(wsel_ref[...], imp_hi) + _dot_nt(wsel_ref[...], imp_lo)
    nb = p_slc.shape[0]
    blk = lax.broadcasted_iota(I32, (nb, tq_n), 0)
    tq_t = t0 + lax.broadcasted_iota(I32, (nb, tq_n), 1)
    cur = tq_t // SLC_BLOCK
    forced = (blk == 0) | (blk == cur) | (blk == cur - 1)
    score = jnp.where(forced, jnp.inf, jnp.where(blk * SLC_BLOCK <= tq_t, p_slc, -jnp.inf))
    n_real = ks_ref.shape[2] // SLC_BLOCK
    cnt = jnp.zeros((nb, tq_n), F32)
    for i in range(n_real):
        ri = score[i:i + 1, :]
        beats = (ri > score) | ((ri == score) & (blk > i))
        cnt = cnt + jnp.where(beats, 1.0, 0.0)
    sel = jnp.where(cnt < n_sel, 1.0, 0.0).T.astype(BF16)

    tk = K_TILE

    def sel_step(c, carry):
        m, l, acc = carry
        k0 = pl.multiple_of(c * tk, tk)
        k = ks_ref[0, 0, pl.ds(k0, tk), :]
        v = vs_ref[0, 0, pl.ds(k0, tk), :]
        kblk = (k0 + lax.broadcasted_iota(I32, (nb, tk), 1)) // SLC_BLOCK
        expand = jnp.where(kblk == lax.broadcasted_iota(I32, (nb, tk), 0), 1.0, 0.0).astype(BF16)
        kpos = k0 + lax.broadcasted_iota(I32, (tq_n, tk), 1)
        tq_s = t0 + lax.broadcasted_iota(I32, (tq_n, tk), 0)
        mask = (_dot(sel, expand) > 0.5) & (kpos <= tq_s)
        s3 = jnp.where(mask[None], _dot_nt(q, k).reshape(r_heads, tq_n, tk), NEG)
        m_new = jnp.maximum(m, jnp.max(s3, axis=-1, keepdims=True))
        alpha = jnp.exp(m - m_new)
        p = jnp.where(mask[None], jnp.exp(s3 - m_new), 0.0)
        l = alpha * l + jnp.sum(p, axis=-1, keepdims=True)
        acc = alpha.reshape(rows, 1) * acc + _dot(p.reshape(rows, tk).astype(BF16), v)
        return m_new, l, acc

    init = (jnp.full((r_heads, tq_n, 1), NEG, F32), jnp.zeros((r_heads, tq_n, 1), F32), jnp.zeros((rows, dh), F32))
    _, l_s, acc_s = lax.fori_loop(0, (t0 + tq_n + tk - 1) // tk, sel_step, init)
    o_s = acc_s / l_s.reshape(rows, 1)

    kwn = WINDOW + tq_n
    start = pl.multiple_of(jnp.maximum(t0 - WINDOW, 0), tq_n)
    kw = kw_ref[0, 0, pl.ds(start, kwn), :]
    vw = vw_ref[0, 0, pl.ds(start, kwn), :]
    kpos_w = start + lax.broadcasted_iota(I32, (tq_n, kwn), 1)
    tq_w = t0 + lax.broadcasted_iota(I32, (tq_n, kwn), 0)
    p_w = _masked_softmax_parts(_dot_nt(q, kw).reshape(r_heads, tq_n, kwn),
                                (kpos_w <= tq_w) & (kpos_w > tq_w - WINDOW))
    o_w = _dot(p_w.reshape(rows, kwn).astype(BF16), vw)

    gates = gate_ref[0]
    lane = lax.broadcasted_iota(I32, gates.shape, 1)
    pick = lambda c: jnp.sum(jnp.where(lane == c, gates, 0.0), axis=-1, keepdims=True)
    for r in range(r_heads):
        base = (g * r_heads + r) * 3
        rs = slice(r * tq_n, (r + 1) * tq_n)
        o = pick(base) * o_c[rs] + pick(base + 1) * o_s[rs] + pick(base + 2) * o_w[rs]
        o_ref[0, :, r * dh:(r + 1) * dh] = o.astype(o_ref.dtype)


def _nsa_attn(q, kc, vc, ks, vs, kw, vw, gate, n_sel):
    bsz, n_heads, t, dh = q.shape
    g = N_KV
    r_heads = n_heads // g
    tq = Q_TILE
    nch = kc.shape[2]
    a = SLC_BLOCK // CMP_STRIDE
    bb = CMP_BLOCK // CMP_STRIDE
    j = jnp.arange(LANES)[:, None]
    c = jnp.arange(nch)[None, :]
    wsel = sum(((c == a * j + m + n_) & (j < t // SLC_BLOCK) & (c < nch - 1)).astype(F32)
               for m in range(a) for n_ in range(bb)).astype(BF16)
    kv_spec = lambda rows: pl.BlockSpec((1, 1, rows, dh), lambda b, gi, qi: (b, gi, 0, 0))
    return pl.pallas_call(
        functools.partial(_nsa_attn_kernel, n_sel=n_sel),
        grid=(bsz, g, t // tq),
        in_specs=[
            pl.BlockSpec((1, r_heads, tq, dh), lambda b, gi, qi: (b, gi, qi, 0)),
            kv_spec(nch), kv_spec(nch), kv_spec(t), kv_spec(t), kv_spec(t), kv_spec(t),
            pl.BlockSpec((1, tq, LANES), lambda b, gi, qi: (b, qi, 0)),
            pl.BlockSpec((LANES, nch), lambda b, gi, qi: (0, 0)),
        ],
        out_specs=pl.BlockSpec((1, tq, r_heads * dh), lambda b, gi, qi: (b, qi, gi)),
        out_shape=jax.ShapeDtypeStruct((bsz, t, n_heads * dh), BF16),
        compiler_params=_params(3),
        name="nsa_attention",
    )(q, kc, vc, ks, vs, kw, vw, gate, wsel)


def _out_mix_kernel(o_ref, w_ref, h_ref, gffn_ref, wcat_ref, br_ref, tri_ref,
                    h2_ref, xn_ref, slab_ref, cnt_ref, carry_ref):
    @pl.when(pl.program_id(0) == 0)
    def _():
        carry_ref[...] = jnp.zeros_like(carry_ref)

    h = h_ref[...] + _dot(o_ref[...], w_ref[...])
    h2_ref[...] = h
    xn, slab = _route(h, gffn_ref[...], wcat_ref, br_ref, tri_ref, carry_ref)
    xn_ref[...] = xn
    slab_ref[...] = slab
    cnt_ref[...] = carry_ref[...]


def _out_mix(o2, w_out, h2, g_ffn, w_rg, b_rg, w_re, b_re):
    n, d = h2.shape
    tm = ROW_TILE
    wcat, br, tri = _router_operands(w_rg, b_rg, w_re, b_re, tm)
    row = pl.BlockSpec((tm, d), lambda i: (i, 0))
    return pl.pallas_call(
        _out_mix_kernel,
        grid=(n // tm,),
        in_specs=[row, _full((d, d)), row, _full((1, d)), _full((d, 2 * LANES)), _full((1, LANES)), _full((tm, tm))],
        out_specs=[row, row, pl.BlockSpec((tm, LANES), lambda i: (i, 0)), _full((1, LANES))],
        out_shape=[jax.ShapeDtypeStruct((n, d), F32), jax.ShapeDtypeStruct((n, d), F32),
                   jax.ShapeDtypeStruct((n, LANES), F32), jax.ShapeDtypeStruct((1, LANES), F32)],
        scratch_shapes=[pltpu.VMEM((1, LANES), F32)],
        compiler_params=_params(1),
        name="nsa_out_mix",
    )(o2, w_out.astype(BF16), h2, g_ffn.reshape(1, d), wcat, br, tri)


def _nsa_layer(h, positions, g_mix, w_in, q_norm, k_norm, cmp_pe, cmp_w1, cmp_w2):
    bsz, t, d = h.shape
    cos_t, sin_t = _rope_table(positions)
    q, kc_raw, vc_raw, ks, vs, kw, vw, gate = _nsa_proj(h, g_mix, w_in, cos_t, sin_t, q_norm, k_norm)
    cmp = _compress(kc_raw, vc_raw, bsz, t, cmp_pe, cmp_w1, cmp_w2, k_norm[0], cos_t, sin_t)
    n_sel = min(N_SEL, t // SLC_BLOCK)
    return _nsa_attn(q, cmp[0], cmp[1], ks, vs, kw, vw, gate.reshape(bsz, t, LANES), n_sel)


def kernel(x, p, positions, g_mix, g_ffn, g_ple, conv_w_pw1, conv_b_pw1, conv_w_dw, conv_b_dw, conv_ln_g, conv_ln_b, conv_w_pw2, conv_b_pw2, nsa_w_in, nsa_q_norm, nsa_k_norm, nsa_cmp_pe, nsa_cmp_w1, nsa_cmp_w2, nsa_w_out, moe_w_rg, moe_b_rg, moe_w_re, moe_b_re, moe_w_gate, moe_w_up, moe_w_down, ple_w_proj, ple_w_gate):
    bsz, t, d = x.shape
    n = bsz * t
    depth = p.shape[0]
    h = x
    for i in range(depth):
        j = i // 2
        route_w = (g_ffn[i], moe_w_rg[i], moe_b_rg[i], moe_w_re[i], moe_b_re[i])
        if i % 2 == 0:
            u = _pw1_glu(h.reshape(n, d), g_mix[i], conv_w_pw1[j], conv_b_pw1[j])
            h, xn, slab, counts = _conv_mix(u.reshape(bsz, t, d), h, conv_w_dw[j], conv_b_dw[j], conv_ln_g[j],
                                            conv_ln_b[j], conv_w_pw2[j], conv_b_pw2[j], *route_w)
        else:
            o = _nsa_layer(h, positions, g_mix[i], nsa_w_in[j], nsa_q_norm[j], nsa_k_norm[j],
                           nsa_cmp_pe[j], nsa_cmp_w1[j], nsa_cmp_w2[j])
            h, xn, slab, counts = _out_mix(o.reshape(n, d), nsa_w_out[j], h.reshape(n, d), *route_w)
        h = _moe_ple(h.reshape(n, d), xn, slab, counts, p[i].reshape(n, -1), g_ple[i],
                     moe_w_gate[i], moe_w_up[i], moe_w_down[i], ple_w_gate[i], ple_w_proj[i]).reshape(bsz, t, d)
    return h
```

```python
import functools

import jax
import jax.numpy as jnp
from jax import lax
from jax.experimental import pallas as pl
from jax.experimental.pallas import tpu as pltpu

F32 = jnp.float32
BF16 = jnp.bfloat16
I32 = jnp.int32

LANES = 128
VMEM_LIMIT = 56 * 1024 * 1024

EPS = 1e-6
NEG = -1e30

CONV_WIDTH = 31
HEAD_DIM = 64
N_KV = 4
CMP_BLOCK = 32
CMP_STRIDE = 16
SLC_BLOCK = 64
N_SEL = 16
WINDOW = 512
ROPE_THETA = 10000.0
N_GROUPS = 4
EXPERTS_PER_GROUP = 8
N_EXPERTS = N_GROUPS * EXPERTS_PER_GROUP

ROW_TILE = 512
CONV_CHUNK = 32
CONV_HALO = 32
EXPERT_ROWS = 256
MOVE_TILE = 256
Q_TILE = 128
K_TILE = 512
ROUTE_FIELDS = 8


def _params(n_grid):
    return pltpu.CompilerParams(dimension_semantics=("arbitrary",) * n_grid,
                                vmem_limit_bytes=VMEM_LIMIT)


def _dot(a, b):
    return jnp.dot(a, b, preferred_element_type=F32)


def _dot_nt(a, b):
    return lax.dot_general(a, b, (((1,), (1,)), ((), ())), preferred_element_type=F32)


def _rms(x, g):
    return x * lax.rsqrt(jnp.mean(x * x, axis=-1, keepdims=True) + EPS) * g


def _full(shape):
    n = len(shape)
    return pl.BlockSpec(shape, lambda *_: (0,) * n)


def _route(h, g_ffn, wcat_ref, br_ref, tri_ref, carry_ref):
    xn = _rms(h, g_ffn)
    hi = xn.astype(BF16)
    lo = (xn - hi.astype(F32)).astype(BF16)
    r1 = _dot(hi, wcat_ref[...])
    r2 = _dot(lo, wcat_ref[:, :LANES])
    logits = r1[:, :LANES] + r1[:, LANES:] + r2 + br_ref[...]

    lane = lax.broadcasted_iota(I32, logits.shape, 1).astype(F32)
    lg = jnp.where(lane < N_GROUPS, logits, -jnp.inf)
    m = jnp.max(lg, axis=-1, keepdims=True)
    grp = jnp.min(jnp.where(lg == m, lane, float(LANES)), axis=-1, keepdims=True)
    pg_sel = 1.0 / jnp.sum(jnp.exp(lg - m), axis=-1, keepdims=True)

    lo_e = N_GROUPS + EXPERTS_PER_GROUP * grp
    le = jnp.where((lane >= lo_e) & (lane < lo_e + EXPERTS_PER_GROUP), logits, -jnp.inf)
    v1 = jnp.max(le, axis=-1, keepdims=True)
    i1 = jnp.min(jnp.where(le == v1, lane, float(LANES)), axis=-1, keepdims=True)
    le2 = jnp.where(lane == i1, -jnp.inf, le)
    v2 = jnp.max(le2, axis=-1, keepdims=True)
    i2 = jnp.min(jnp.where(le2 == v2, lane, float(LANES)), axis=-1, keepdims=True)
    e21 = jnp.exp(v2 - v1)
    w1 = pg_sel / (1.0 + e21)
    w2 = pg_sel * e21 / (1.0 + e21)
    e1 = i1 - N_GROUPS
    e2 = i2 - N_GROUPS

    oh1 = jnp.where(lane == e1, 1.0, 0.0)
    oh2 = jnp.where(lane == e2, 1.0, 0.0)
    c1 = _dot(tri_ref[...], oh1.astype(BF16))
    c2 = _dot(tri_ref[...], oh2.astype(BF16))
    carry = carry_ref[...]
    tot1 = jnp.sum(oh1, axis=0, keepdims=True)
    tot2 = jnp.sum(oh2, axis=0, keepdims=True)
    rank1 = jnp.sum(oh1 * (c1 + carry), axis=-1, keepdims=True)
    rank2 = jnp.sum(oh2 * (c2 + carry + tot1), axis=-1, keepdims=True)
    carry_ref[...] = carry + tot1 + tot2

    slab = jnp.where(lane == 0, e1, jnp.where(lane == 1, e2, jnp.where(lane == 2, w1, jnp.where(
        lane == 3, w2, jnp.where(lane == 4, rank1, jnp.where(lane == 5, rank2, 0.0))))))
    return xn, slab


def _router_operands(w_rg, b_rg, w_re, b_re, tm):
    d = w_rg.shape[0]
    w = jnp.zeros((d, LANES), F32).at[:, :N_GROUPS].set(w_rg).at[:, N_GROUPS:N_GROUPS + N_EXPERTS].set(w_re)
    w_hi = w.astype(BF16)
    w_lo = (w - w_hi.astype(F32)).astype(BF16)
    wcat = jnp.concatenate([w_hi, w_lo], axis=1)
    br = jnp.zeros((1, LANES), F32).at[0, :N_GROUPS].set(b_rg).at[0, N_GROUPS:N_GROUPS + N_EXPERTS].set(b_re)
    r = lax.broadcasted_iota(I32, (tm, tm), 0)
    c = lax.broadcasted_iota(I32, (tm, tm), 1)
    tri = (c < r).astype(BF16)
    return wcat, br, tri


def _pw1_glu_kernel(x_ref, g_ref, w_ref, b_ref, o_ref, *, chunk):
    xn = _rms(x_ref[...], g_ref[...]).astype(BF16)
    d = o_ref.shape[1]
    for j in range(d // chunk):
        sa = slice(j * chunk, (j + 1) * chunk)
        sg = slice(d + j * chunk, d + (j + 1) * chunk)
        a = _dot(xn, w_ref[:, sa]) + b_ref[:, sa]
        g = _dot(xn, w_ref[:, sg]) + b_ref[:, sg]
        o_ref[:, sa] = a * jax.nn.sigmoid(g)


def _pw1_glu(x2, g, w, b):
    n, d = x2.shape
    tm = ROW_TILE
    return pl.pallas_call(
        functools.partial(_pw1_glu_kernel, chunk=512),
        grid=(n // tm,),
        in_specs=[pl.BlockSpec((tm, d), lambda i: (i, 0)), _full((1, d)), _full((d, 2 * d)), _full((1, 2 * d))],
        out_specs=pl.BlockSpec((tm, d), lambda i: (i, 0)),
        out_shape=jax.ShapeDtypeStruct((n, d), F32),
        compiler_params=_params(1),
        name="pw1_glu",
    )(x2, g.reshape(1, d), w.astype(BF16), b.reshape(1, 2 * d))


def _conv_mix_kernel(u_ref, halo_ref, x_ref, wdw_ref, bdw_ref, lng_ref, lnb_ref, w2_ref, b2_ref,
                     gffn_ref, wcat_ref, br_ref, tri_ref,
                     h_ref, xn_ref, slab_ref, slabt_ref, cnt_ref,
                     ext_ref, act_ref, carry_ref):
    b = pl.program_id(0)
    i = pl.program_id(1)
    tt = u_ref.shape[1]

    @pl.when((b == 0) & (i == 0))
    def _():
        carry_ref[...] = jnp.zeros_like(carry_ref)

    halo = halo_ref[0]
    ext_ref[0:CONV_HALO, :] = jnp.where(i == 0, jnp.zeros_like(halo), halo)
    ext_ref[CONV_HALO:, :] = u_ref[0]
    lead = CONV_HALO - (CONV_WIDTH - 1)

    def chunk(c, carry):
        base = pl.multiple_of(c * CONV_CHUNK, CONV_CHUNK)
        win = ext_ref[pl.ds(base, CONV_CHUNK + CONV_HALO), :]
        acc = jnp.zeros((CONV_CHUNK, win.shape[1]), F32)
        for k in range(CONV_WIDTH):
            acc = acc + wdw_ref[k:k + 1, :] * win[lead + k:lead + k + CONV_CHUNK, :]
        acc = acc + bdw_ref[...]
        mu = jnp.mean(acc, axis=-1, keepdims=True)
        cen = acc - mu
        var = jnp.mean(cen * cen, axis=-1, keepdims=True)
        y = cen * lax.rsqrt(var + EPS) * lng_ref[...] + lnb_ref[...]
        act_ref[pl.ds(base, CONV_CHUNK), :] = (y * jax.nn.sigmoid(y)).astype(BF16)
        return carry

    lax.fori_loop(0, tt // CONV_CHUNK, chunk, 0)

    h = x_ref[0] + _dot(act_ref[...], w2_ref[...]) + b2_ref[...]
    h_ref[0] = h
    xn, slab = _route(h, gffn_ref[...], wcat_ref, br_ref, tri_ref, carry_ref)
    xn_ref[...] = xn
    slab_ref[...] = slab
    slabt_ref[...] = slab.T[:ROUTE_FIELDS]
    cnt_ref[...] = carry_ref[...]


def _conv_mix(u, x, w_dw, b_dw, ln_g, ln_b, w_pw2, b_pw2, g_ffn, w_rg, b_rg, w_re, b_re):
    bsz, t, d = x.shape
    tt = ROW_TILE
    nt = t // tt
    n = bsz * t
    wcat, br, tri = _router_operands(w_rg, b_rg, w_re, b_re, tt)
    wdw = jnp.zeros((CONV_HALO, d), F32).at[:CONV_WIDTH].set(w_dw)
    hb = tt // CONV_HALO
    row = lambda v: v.reshape(1, d)
    return pl.pallas_call(
        _conv_mix_kernel,
        grid=(bsz, nt),
        in_specs=[
            pl.BlockSpec((1, tt, d), lambda b, i: (b, i, 0)),
            pl.BlockSpec((1, CONV_HALO, d), lambda b, i: (b, jnp.maximum(i * hb - 1, 0), 0)),
            pl.BlockSpec((1, tt, d), lambda b, i: (b, i, 0)),
            _full((CONV_HALO, d)), _full((1, d)), _full((1, d)), _full((1, d)),
            _full((d, d)), _full((1, d)), _full((1, d)),
            _full((d, 2 * LANES)), _full((1, LANES)), _full((tt, tt)),
        ],
        out_specs=[
            pl.BlockSpec((1, tt, d), lambda b, i: (b, i, 0)),
            pl.BlockSpec((tt, d), lambda b, i: (b * nt + i, 0)),
            pl.BlockSpec((tt, LANES), lambda b, i: (b * nt + i, 0)),
            pl.BlockSpec((ROUTE_FIELDS, tt), lambda b, i: (0, b * nt + i)),
            _full((1, LANES)),
        ],
        out_shape=[
            jax.ShapeDtypeStruct((bsz, t, d), F32),
            jax.ShapeDtypeStruct((n, d), F32),
            jax.ShapeDtypeStruct((n, LANES), F32),
            jax.ShapeDtypeStruct((ROUTE_FIELDS, n), F32),
            jax.ShapeDtypeStruct((1, LANES), F32),
        ],
        scratch_shapes=[pltpu.VMEM((CONV_HALO + tt, d), F32), pltpu.VMEM((tt, d), BF16),
                        pltpu.VMEM((1, LANES), F32)],
        compiler_params=_params(2),
        name="conv_mix",
    )(u, u, x, wdw, row(b_dw), row(ln_g), row(ln_b), w_pw2.astype(BF16), row(b_pw2), row(g_ffn), wcat, br, tri)


def _dispatch_kernel(dest_ref, xn_ref, xs_in_ref, xs_ref, sem):
    del xs_in_ref
    tm = xn_ref.shape[0]
    base = pl.program_id(0) * tm
    n = pl.num_programs(0) * tm

    def row_copy(r, d):
        return pltpu.make_async_copy(xn_ref.at[pl.ds(r, 1)], xs_ref.at[pl.ds(d, 1)], sem)

    def issue(r, carry):
        row_copy(r, dest_ref[base + r]).start()
        row_copy(r, dest_ref[n + base + r]).start()
        return carry

    lax.fori_loop(0, tm, issue, 0, unroll=8)

    def drain(r, carry):
        row_copy(r, 0).wait()
        row_copy(r, 0).wait()
        return carry

    lax.fori_loop(0, tm, drain, 0, unroll=8)


def _dispatch(dest_flat, xn, n_slots):
    n, d = xn.shape
    tm = MOVE_TILE
    return pl.pallas_call(
        _dispatch_kernel,
        grid_spec=pltpu.PrefetchScalarGridSpec(
            num_scalar_prefetch=1,
            grid=(n // tm,),
            in_specs=[pl.BlockSpec((tm, d), lambda i, dest: (i, 0)), pl.BlockSpec(memory_space=pl.ANY)],
            out_specs=pl.BlockSpec(memory_space=pl.ANY),
            scratch_shapes=[pltpu.SemaphoreType.DMA(())],
        ),
        out_shape=jax.ShapeDtypeStruct((n_slots, d), F32),
        input_output_aliases={2: 0},
        compiler_params=_params(1),
        name="moe_dispatch",
    )(dest_flat, xn, jnp.zeros((n_slots, d), F32))


def _expert_kernel(be_ref, nu_ref, xs_ref, wg_ref, wu_ref, wd_ref, y_ref, wgb_ref, wub_ref, wdb_ref):
    i = pl.program_id(0)
    changed = (i == 0) | (be_ref[i] != be_ref[jnp.maximum(i - 1, 0)])

    @pl.when(changed)
    def _():
        wgb_ref[...] = wg_ref[0, 0].astype(BF16)
        wub_ref[...] = wu_ref[0, 0].astype(BF16)
        wdb_ref[...] = wd_ref[0, 0].astype(BF16)

    @pl.when(i < nu_ref[0])
    def _():
        x = xs_ref[...].astype(BF16)
        a = _dot(x, wgb_ref[...])
        u = _dot(x, wub_ref[...])
        hid = (a * jax.nn.sigmoid(a) * u).astype(BF16)
        y_ref[...] = _dot(hid, wdb_ref[...])

    @pl.when(i >= nu_ref[0])
    def _():
        y_ref[...] = jnp.zeros_like(y_ref)


def _experts(blk_e, n_used, xs, layer, w_gate, w_up, w_down):
    p_rows, d = xs.shape
    f = w_gate.shape[3]
    be = EXPERT_ROWS
    return pl.pallas_call(
        _expert_kernel,
        grid_spec=pltpu.PrefetchScalarGridSpec(
            num_scalar_prefetch=2,
            grid=(p_rows // be,),
            in_specs=[
                pl.BlockSpec((be, d), lambda i, e, nu: (i, 0)),
                pl.BlockSpec((1, 1, d, f), lambda i, e, nu: (layer, e[i], 0, 0)),
                pl.BlockSpec((1, 1, d, f), lambda i, e, nu: (layer, e[i], 0, 0)),
                pl.BlockSpec((1, 1, f, d), lambda i, e, nu: (layer, e[i], 0, 0)),
            ],
            out_specs=pl.BlockSpec((be, d), lambda i, e, nu: (i, 0)),
            scratch_shapes=[pltpu.VMEM((d, f), BF16), pltpu.VMEM((d, f), BF16), pltpu.VMEM((f, d), BF16)],
        ),
        out_shape=jax.ShapeDtypeStruct((p_rows, d), F32),
        compiler_params=_params(1),
        name="moe_experts",
    )(blk_e, n_used, xs, w_gate, w_up, w_down)


def _combine_ple_kernel(dest_ref, h_ref, slab_ref, p_ref, gple_ref, wg_ref, wp_ref, y_hbm,
                        o_ref, ybuf, sems):
    tm = h_ref.shape[0]
    i = pl.program_id(0)
    n_steps = pl.num_programs(0)
    n = n_steps * tm
    slot = i % 2

    def row_copy(step_slot, k, r, d):
        return pltpu.make_async_copy(y_hbm.at[pl.ds(d, 1)], ybuf.at[step_slot, k, pl.ds(r, 1)],
                                     sems.at[step_slot])

    def issue(step, step_slot):
        base = step * tm

        def body(r, carry):
            row_copy(step_slot, 0, r, dest_ref[base + r]).start()
            row_copy(step_slot, 1, r, dest_ref[n + base + r]).start()
            return carry

        lax.fori_loop(0, tm, body, 0, unroll=8)

    @pl.when(i == 0)
    def _():
        issue(0, 0)

    @pl.when(i + 1 < n_steps)
    def _():
        issue(i + 1, 1 - slot)

    def drain(r, carry):
        row_copy(slot, 0, r, 0).wait()
        row_copy(slot, 1, r, 0).wait()
        return carry

    lax.fori_loop(0, tm, drain, 0, unroll=8)

    slab = slab_ref[...]
    lane = lax.broadcasted_iota(I32, slab.shape, 1)
    w0 = jnp.sum(jnp.where(lane == 2, slab, 0.0), axis=-1, keepdims=True)
    w1 = jnp.sum(jnp.where(lane == 3, slab, 0.0), axis=-1, keepdims=True)
    hm = h_ref[...] + w0 * ybuf[slot, 0] + w1 * ybuf[slot, 1]
    gate = jax.nn.sigmoid(_dot(_rms(hm, gple_ref[...]).astype(BF16), wg_ref[...]))
    o_ref[...] = hm + gate * _dot(p_ref[0].astype(BF16), wp_ref[...])


def _combine_ple(dest_flat, h2, slab, layer, p3, g_ple, w_gate, w_proj, y):
    n, d = h2.shape
    dp = p3.shape[2]
    tm = MOVE_TILE
    return pl.pallas_call(
        _combine_ple_kernel,
        grid_spec=pltpu.PrefetchScalarGridSpec(
            num_scalar_prefetch=1,
            grid=(n // tm,),
            in_specs=[
                pl.BlockSpec((tm, d), lambda i, dest: (i, 0)),
                pl.BlockSpec((tm, LANES), lambda i, dest: (i, 0)),
                pl.BlockSpec((1, tm, dp), lambda i, dest: (layer, i, 0)),
                pl.BlockSpec((1, d), lambda i, dest: (0, 0)),
                pl.BlockSpec((d, d), lambda i, dest: (0, 0)),
                pl.BlockSpec((dp, d), lambda i, dest: (0, 0)),
                pl.BlockSpec(memory_space=pl.ANY),
            ],
            out_specs=pl.BlockSpec((tm, d), lambda i, dest: (i, 0)),
            scratch_shapes=[pltpu.VMEM((2, 2, tm, d), F32), pltpu.SemaphoreType.DMA((2,))],
        ),
        out_shape=jax.ShapeDtypeStruct((n, d), F32),
        compiler_params=_params(1),
        name="moe_combine_ple",
    )(dest_flat, h2, slab, p3, g_ple.reshape(1, d), w_gate.astype(BF16), w_proj.astype(BF16), y)


def _moe_ple(h2, xn, slab, slab_t, counts, layer, p3, g_ple, w_gate_e, w_up_e, w_down_e, ple_w_gate, ple_w_proj):
    n, d = h2.shape
    be = EXPERT_ROWS
    n_blocks = (2 * n) // be + N_EXPERTS
    cnt = counts[0, :N_EXPERTS].astype(I32)
    padded = (cnt + be - 1) // be * be
    ends = jnp.cumsum(padded)
    starts = ends - padded
    eid = slab_t[0:2].astype(I32)
    rank = slab_t[4:6].astype(I32)
    dest_flat = (starts[eid] + rank).reshape(2 * n)
    block_start = jnp.arange(n_blocks, dtype=I32) * be
    blk_e = jnp.minimum(jnp.sum((ends[None, :] <= block_start[:, None]).astype(I32), axis=1), N_EXPERTS - 1)
    n_used = (ends[-1:] // be).astype(I32)
    xs = _dispatch(dest_flat, xn, n_blocks * be)
    y = _experts(blk_e, n_used, xs, layer, w_gate_e, w_up_e, w_down_e)
    return _combine_ple(dest_flat, h2, slab, layer, p3, g_ple, ple_w_gate, ple_w_proj, y)


def _rope_table_kernel(pos_ref, inv_ref, sign_ref, cos_ref, sin_ref):
    ang = pos_ref[...] * inv_ref[...]
    cos_ref[...] = jnp.cos(ang)
    sin_ref[...] = jnp.sin(ang) * sign_ref[...]


def _rope_table(positions):
    n = positions.size
    tm = ROW_TILE
    half = HEAD_DIM // 2
    lane = jnp.arange(LANES)
    inv = 1.0 / (ROPE_THETA ** ((lane % half).astype(F32) / half))
    sign = jnp.where(lane % HEAD_DIM < half, -1.0, 1.0).astype(F32)
    return pl.pallas_call(
        _rope_table_kernel,
        grid=(n // tm,),
        in_specs=[pl.BlockSpec((tm, 1), lambda i: (i, 0)), _full((1, LANES)), _full((1, LANES))],
        out_specs=[pl.BlockSpec((tm, LANES), lambda i: (i, 0))] * 2,
        out_shape=[jax.ShapeDtypeStruct((n, LANES), F32)] * 2,
        compiler_params=_params(1),
        name="rope_table",
    )(positions.astype(F32).reshape(n, 1), inv.reshape(1, LANES), sign.reshape(1, LANES))


def _head_norm_rope(z, gain, seg_ref, cos, sin):
    z2 = z * z
    hi = z2.astype(BF16)
    lo = (z2 - hi.astype(F32)).astype(BF16)
    ssq = _dot(hi, seg_ref[...]) + _dot(lo, seg_ref[...])
    zn = z * lax.rsqrt(ssq * (1.0 / HEAD_DIM) + EPS) * gain
    width = z.shape[1]
    half = HEAD_DIM // 2
    lane = lax.broadcasted_iota(I32, z.shape, 1)
    rot = jnp.where((lane & half) == 0, pltpu.roll(zn, width - half, 1), pltpu.roll(zn, half, 1))
    return zn * cos + rot * sin


def _nsa_proj_kernel(h_ref, g_ref, w_ref, cos_ref, sin_ref, qn_ref, kns_ref, knw_ref, seg_ref,
                     q_ref, kc_ref, vc_ref, ks_ref, vs_ref, kw_ref, vw_ref, gate_ref):
    d = h_ref.shape[1]
    kd = N_KV * HEAD_DIM
    xn = _rms(h_ref[...], g_ref[...]).astype(BF16)
    cos = jnp.concatenate([cos_ref[...], cos_ref[...]], axis=1)
    sin = jnp.concatenate([sin_ref[...], sin_ref[...]], axis=1)
    scale = HEAD_DIM ** -0.5

    def heads_out(ref, first, z):
        ones = jnp.ones((z.shape[0], HEAD_DIM), ref.dtype)
        for r in range(N_KV):
            zr = z[:, r * HEAD_DIM:(r + 1) * HEAD_DIM].astype(ref.dtype)
            ref[0, first + r] = zr if ref.shape[-1] == HEAD_DIM else jnp.concatenate([zr, ones], axis=1)

    for j in range(d // kd):
        z = _dot(xn, w_ref[:, j * kd:(j + 1) * kd])
        heads_out(q_ref, N_KV * j, _head_norm_rope(z, qn_ref[...], seg_ref, cos, sin) * scale)
    col = lambda c: _dot(xn, w_ref[:, d + c * kd:d + (c + 1) * kd])
    kc_ref[...] = col(0)
    vc_ref[...] = col(1)
    heads_out(ks_ref, 0, _head_norm_rope(col(2), kns_ref[...], seg_ref, cos, sin))
    heads_out(vs_ref, 0, col(3))
    heads_out(kw_ref, 0, _head_norm_rope(col(4), knw_ref[...], seg_ref, cos, sin))
    heads_out(vw_ref, 0, col(5))
    gate_ref[...] = jax.nn.sigmoid(_dot(xn, w_ref[:, d + 6 * kd:]))


def _nsa_proj(h, g_mix, w_in, cos_t, sin_t, q_norm, k_norm):
    bsz, t, d = h.shape
    n = bsz * t
    tm = ROW_TILE
    nt = t // tm
    kd = N_KV * HEAD_DIM
    n_heads = d // HEAD_DIM
    n_in = w_in.shape[1]
    w_pad = jnp.zeros((d, d + 6 * kd + LANES), F32).at[:, :n_in].set(w_in).astype(BF16)
    tile4 = lambda v: jnp.tile(v, kd // HEAD_DIM).reshape(1, kd)
    idx = jnp.arange(kd) // HEAD_DIM
    seg = (idx[:, None] == idx[None, :]).astype(BF16)
    head_spec = lambda nh, w=HEAD_DIM: pl.BlockSpec((1, nh, tm, w), lambda i: (i // nt, 0, i % nt, 0))
    head_shape = lambda nh, w=HEAD_DIM: jax.ShapeDtypeStruct((bsz, nh, t, w), BF16)
    flat_spec = lambda w: pl.BlockSpec((tm, w), lambda i: (i, 0))
    vw2 = 2 * HEAD_DIM
    return pl.pallas_call(
        _nsa_proj_kernel,
        grid=(n // tm,),
        in_specs=[flat_spec(d), _full((1, d)), _full(w_pad.shape), flat_spec(LANES), flat_spec(LANES),
                  _full((1, kd)), _full((1, kd)), _full((1, kd)), _full((kd, kd))],
        out_specs=[head_spec(n_heads), flat_spec(kd), flat_spec(kd), head_spec(N_KV), head_spec(N_KV, vw2),
                   head_spec(N_KV), head_spec(N_KV, vw2), flat_spec(LANES)],
        out_shape=[head_shape(n_heads), jax.ShapeDtypeStruct((n, kd), F32), jax.ShapeDtypeStruct((n, kd), F32),
                   head_shape(N_KV), head_shape(N_KV, vw2), head_shape(N_KV), head_shape(N_KV, vw2),
                   jax.ShapeDtypeStruct((n, LANES), F32)],
        compiler_params=_params(1),
        name="nsa_proj",
    )(h.reshape(n, d), g_mix.reshape(1, d), w_pad, cos_t, sin_t,
      tile4(q_norm), tile4(k_norm[1]), tile4(k_norm[2]), seg)


def _compress_kernel(x_ref, pe_ref, w1_ref, w2_ref, kn_ref, cos_ref, sin_ref, o_ref):
    which = pl.program_id(1)
    half_w = x_ref.shape[-1]
    x = x_ref[0, 0, 0]
    first = _dot((x + pe_ref[0, 0:1, :]).astype(BF16), w1_ref[0, :half_w, :])
    second = _dot((x + pe_ref[0, 1:2, :]).astype(BF16), w1_ref[0, half_w:, :])
    n_chunks = x.shape[0]
    hid = first + pltpu.roll(second, n_chunks - 1, 0)
    c = _dot(jax.nn.gelu(hid, approximate=True).astype(BF16), w2_ref[0])
    half = HEAD_DIM // 2
    cn = _rms(c, kn_ref[...])
    rot = jnp.concatenate([cn[:, half:], cn[:, :half]], axis=1)
    ck = cn * cos_ref[0] + rot * sin_ref[0]
    o_ref[0, 0, 0] = jnp.where(which == 0, ck, c).astype(o_ref.dtype)


def _compress(kc_raw, vc_raw, bsz, t, pe, w1, w2, k_norm0, cos_t, sin_t):
    g, dh, st = N_KV, HEAD_DIM, CMP_STRIDE
    nch = t // st
    wide = st * dh
    hidden = w1.shape[-1]

    def chunks(a):
        return a.reshape(bsz, nch, st, g, dh).transpose(0, 3, 1, 2, 4).reshape(bsz, g, nch, wide)

    x = jnp.stack([chunks(kc_raw), chunks(vc_raw)])
    pe2 = pe.reshape(2, 2, wide)
    last = lambda tab: jnp.pad(tab.reshape(bsz, t, LANES)[:, CMP_BLOCK - 1::st, :dh], ((0, 0), (0, 1), (0, 0)))
    return pl.pallas_call(
        _compress_kernel,
        grid=(bsz, 2, g),
        in_specs=[
            pl.BlockSpec((1, 1, 1, nch, wide), lambda b, w, gi: (w, b, gi, 0, 0)),
            pl.BlockSpec((1, 2, wide), lambda b, w, gi: (w, 0, 0)),
            pl.BlockSpec((1, 2 * wide, hidden), lambda b, w, gi: (w, 0, 0)),
            pl.BlockSpec((1, hidden, dh), lambda b, w, gi: (w, 0, 0)),
            pl.BlockSpec((1, dh), lambda b, w, gi: (0, 0)),
            pl.BlockSpec((1, nch, dh), lambda b, w, gi: (b, 0, 0)),
            pl.BlockSpec((1, nch, dh), lambda b, w, gi: (b, 0, 0)),
        ],
        out_specs=pl.BlockSpec((1, 1, 1, nch, dh), lambda b, w, gi: (w, b, gi, 0, 0)),
        out_shape=jax.ShapeDtypeStruct((2, bsz, g, nch, dh), BF16),
        compiler_params=_params(3),
        name="nsa_compress",
    )(x, pe2, w1.astype(BF16), w2.astype(BF16), k_norm0.reshape(1, dh), last(cos_t), last(sin_t))


def _attend_chunk(q, k, v, bias, m_ref, acc_ref, r_heads, tq_n):
    tk = k.shape[0]
    s = _dot_nt(q, k)
    m_all = m_ref[...]
    acc_all = acc_ref[...]
    m_out, scaled, probs = [], [], []
    for r in range(r_heads):
        rs = slice(r * tq_n, (r + 1) * tq_n)
        sr = s[rs] + bias
        m_old = m_all[rs]
        m_new = jnp.maximum(m_old, jnp.max(sr, axis=-1, keepdims=True))
        m_wide = m_new if tk == m_new.shape[1] else jnp.concatenate([m_new] * (tk // m_new.shape[1]), axis=1)
        probs.append(jnp.exp(sr - m_wide).astype(BF16))
        m_out.append(m_new)
        scaled.append(jnp.exp(m_old - m_new) * acc_all[rs])
    m_ref[...] = jnp.concatenate(m_out, axis=0)
    acc_ref[...] = jnp.concatenate(scaled, axis=0) + _dot(jnp.concatenate(probs, axis=0), v)


def _nsa_attn_kernel(q_ref, kc_ref, vc_ref, ks_ref, vs_ref, kw_ref, vw_ref, gate_ref, wsel_ref, e_ref, o_ref,
                     m_ref, acc_ref, og_ref, *, n_sel):
    g = pl.program_id(1)
    qi = pl.program_id(2)
    r_heads, tq_n, dh = q_ref.shape[1], q_ref.shape[2], q_ref.shape[3]
    rows = r_heads * tq_n
    t0 = qi * tq_n
    tk = K_TILE
    q = q_ref[0].reshape(rows, dh)
    head_rows = [slice(r * tq_n, (r + 1) * tq_n) for r in range(r_heads)]

    gates = gate_ref[0]
    gate_lane = lax.broadcasted_iota(I32, gates.shape, 1)

    def gate(r, branch):
        col = (g * r_heads + r) * 3 + branch
        return jnp.sum(jnp.where(gate_lane == col, gates, 0.0), axis=-1, keepdims=True)

    def restart():
        m_ref[...] = jnp.full(m_ref.shape, NEG, F32)
        acc_ref[...] = jnp.zeros(acc_ref.shape, F32)

    def add_branch(branch):
        acc = acc_ref[...]
        o = acc * (1.0 / pltpu.roll(acc, dh, 1))
        og_ref[...] += jnp.concatenate([gate(r, branch) * o[rs, :dh] for r, rs in enumerate(head_rows)], axis=0)

    kc = kc_ref[0, 0]
    nc = kc.shape[0]
    tq_c = t0 + lax.broadcasted_iota(I32, (tq_n, nc), 0)
    mask_c = CMP_STRIDE * lax.broadcasted_iota(I32, (tq_n, nc), 1) + (CMP_BLOCK - 1) <= tq_c
    s_c = _dot_nt(q, kc)
    imp = jnp.zeros((tq_n, nc), F32)
    probs = []
    for rs in head_rows:
        sr = jnp.where(mask_c, s_c[rs], NEG)
        e = jnp.where(mask_c, jnp.exp(sr - jnp.max(sr, axis=-1, keepdims=True)), 0.0)
        l = jnp.sum(e, axis=-1, keepdims=True)
        p = e * jnp.where(l > 0.0, 1.0 / l, 0.0)
        imp = imp + p
        probs.append(p.astype(BF16))
    o_c = _dot(jnp.concatenate(probs, axis=0), vc_ref[0, 0])
    og_ref[...] = jnp.concatenate([gate(r, 0) * o_c[rs] for r, rs in enumerate(head_rows)], axis=0)

    imp_hi = imp.astype(BF16)
    imp_lo = (imp - imp_hi.astype(F32)).astype(BF16)
    p_slc = _dot_nt(wsel_ref[...], imp_hi) + _dot_nt(wsel_ref[...], imp_lo)
    nsb = p_slc.shape[0]
    blk = lax.broadcasted_iota(I32, (nsb, tq_n), 0)
    tq_t = t0 + lax.broadcasted_iota(I32, (nsb, tq_n), 1)
    cur = tq_t // SLC_BLOCK
    forced = (blk == 0) | (blk == cur) | (blk == cur - 1)
    score = jnp.where(forced, jnp.inf, jnp.where(blk * SLC_BLOCK <= tq_t, p_slc, -jnp.inf))
    cnt = jnp.zeros((nsb, tq_n), F32)
    for i in range(nsb):
        ri = score[i:i + 1, :]
        beats = (ri > score) | ((ri == score) & (blk > i))
        cnt = cnt + jnp.where(beats, 1.0, 0.0)
    dropped = jnp.where(cnt < n_sel, 0.0, 1.0)
    dropped = jnp.concatenate([dropped, jnp.zeros((LANES - nsb, tq_n), F32)], axis=0).T.astype(BF16)

    kwn = WINDOW + tq_n
    start = pl.multiple_of(jnp.maximum(t0 - WINDOW, 0), tq_n)
    kpos_w = start + lax.broadcasted_iota(I32, (tq_n, kwn), 1)
    tq_w = t0 + lax.broadcasted_iota(I32, (tq_n, kwn), 0)
    bias_w = jnp.where((kpos_w <= tq_w) & (kpos_w > tq_w - WINDOW), 0.0, NEG)
    s_w = _dot_nt(q, kw_ref[0, 0, pl.ds(start, kwn), :])
    probs = []
    for rs in head_rows:
        sr = s_w[rs] + bias_w
        probs.append(jnp.exp(sr - jnp.max(sr, axis=-1, keepdims=True)).astype(BF16))
    acc_w = _dot(jnp.concatenate(probs, axis=0), vw_ref[0, 0, pl.ds(start, kwn), :])
    o_w = acc_w * (1.0 / pltpu.roll(acc_w, dh, 1))
    og_ref[...] += jnp.concatenate([gate(r, 2) * o_w[rs, :dh] for r, rs in enumerate(head_rows)], axis=0)

    restart()

    def sel_step(c, carry):
        k0 = pl.multiple_of(c * tk, tk)
        kpos = k0 + lax.broadcasted_iota(I32, (tq_n, tk), 1)
        tq_s = t0 + lax.broadcasted_iota(I32, (tq_n, tk), 0)
        bias = jnp.where(kpos <= tq_s, _dot(dropped, e_ref[c]), NEG)
        _attend_chunk(q, ks_ref[0, 0, pl.ds(k0, tk), :], vs_ref[0, 0, pl.ds(k0, tk), :], bias,
                      m_ref, acc_ref, r_heads, tq_n)
        return carry

    lax.fori_loop(0, (t0 + tq_n + tk - 1) // tk, sel_step, 0)
    add_branch(1)

    for r, rs in enumerate(head_rows):
        o_ref[0, :, r * dh:(r + 1) * dh] = og_ref[rs].astype(o_ref.dtype)


def _nsa_attn(q, kc, vc, ks, vs, kw, vw, gate, n_sel):
    bsz, n_heads, t, dh = q.shape
    g = N_KV
    r_heads = n_heads // g
    tq = Q_TILE
    nch = kc.shape[2]
    a = SLC_BLOCK // CMP_STRIDE
    bb = CMP_BLOCK // CMP_STRIDE
    nsb = t // SLC_BLOCK
    j = jnp.arange(nsb)[:, None]
    c = jnp.arange(nch)[None, :]
    wsel = sum(((c == a * j + m + n_) & (c < nch - 1)).astype(F32)
               for m in range(a) for n_ in range(bb)).astype(BF16)
    tk = K_TILE
    key_blk = (jnp.arange(t) // SLC_BLOCK).reshape(t // tk, 1, tk)
    e = jnp.where(key_blk == jnp.arange(LANES).reshape(1, LANES, 1), NEG, 0.0).astype(BF16)
    rows = r_heads * tq
    kv_spec = lambda n_rows, w=dh: pl.BlockSpec((1, 1, n_rows, w), lambda b, gi, qi: (b, gi, 0, 0))
    return pl.pallas_call(
        functools.partial(_nsa_attn_kernel, n_sel=n_sel),
        grid=(bsz, g, t // tq),
        in_specs=[
            pl.BlockSpec((1, r_heads, tq, dh), lambda b, gi, qi: (b, gi, qi, 0)),
            kv_spec(nch), kv_spec(nch), kv_spec(t), kv_spec(t, 2 * dh), kv_spec(t), kv_spec(t, 2 * dh),
            pl.BlockSpec((1, tq, LANES), lambda b, gi, qi: (b, qi, 0)),
            pl.BlockSpec((nsb, nch), lambda b, gi, qi: (0, 0)),
            pl.BlockSpec(e.shape, lambda b, gi, qi: (0, 0, 0)),
        ],
        out_specs=pl.BlockSpec((1, tq, r_heads * dh), lambda b, gi, qi: (b, qi, gi)),
        out_shape=jax.ShapeDtypeStruct((bsz, t, n_heads * dh), BF16),
        scratch_shapes=[pltpu.VMEM((rows, 2 * dh), F32), pltpu.VMEM((rows, 2 * dh), F32),
                        pltpu.VMEM((rows, dh), F32)],
        compiler_params=_params(3),
        name="nsa_attention",
    )(q, kc, vc, ks, vs, kw, vw, gate, wsel, e)


def _out_mix_kernel(o_ref, w_ref, h_ref, gffn_ref, wcat_ref, br_ref, tri_ref,
                    h2_ref, xn_ref, slab_ref, slabt_ref, cnt_ref, carry_ref):
    @pl.when(pl.program_id(0) == 0)
    def _():
        carry_ref[...] = jnp.zeros_like(carry_ref)

    h = h_ref[...] + _dot(o_ref[...], w_ref[...])
    h2_ref[...] = h
    xn, slab = _route(h, gffn_ref[...], wcat_ref, br_ref, tri_ref, carry_ref)
    xn_ref[...] = xn
    slab_ref[...] = slab
    slabt_ref[...] = slab.T[:ROUTE_FIELDS]
    cnt_ref[...] = carry_ref[...]


def _out_mix(o2, w_out, h2, g_ffn, w_rg, b_rg, w_re, b_re):
    n, d = h2.shape
    tm = ROW_TILE
    wcat, br, tri = _router_operands(w_rg, b_rg, w_re, b_re, tm)
    row = pl.BlockSpec((tm, d), lambda i: (i, 0))
    return pl.pallas_call(
        _out_mix_kernel,
        grid=(n // tm,),
        in_specs=[row, _full((d, d)), row, _full((1, d)), _full((d, 2 * LANES)), _full((1, LANES)), _full((tm, tm))],
        out_specs=[row, row, pl.BlockSpec((tm, LANES), lambda i: (i, 0)),
                   pl.BlockSpec((ROUTE_FIELDS, tm), lambda i: (0, i)), _full((1, LANES))],
        out_shape=[jax.ShapeDtypeStruct((n, d), F32), jax.ShapeDtypeStruct((n, d), F32),
                   jax.ShapeDtypeStruct((n, LANES), F32), jax.ShapeDtypeStruct((ROUTE_FIELDS, n), F32),
                   jax.ShapeDtypeStruct((1, LANES), F32)],
        scratch_shapes=[pltpu.VMEM((1, LANES), F32)],
        compiler_params=_params(1),
        name="nsa_out_mix",
    )(o2, w_out.astype(BF16), h2, g_ffn.reshape(1, d), wcat, br, tri)


def _nsa_layer(h, positions, g_mix, w_in, q_norm, k_norm, cmp_pe, cmp_w1, cmp_w2):
    bsz, t, d = h.shape
    cos_t, sin_t = _rope_table(positions)
    q, kc_raw, vc_raw, ks, vs, kw, vw, gate = _nsa_proj(h, g_mix, w_in, cos_t, sin_t, q_norm, k_norm)
    cmp = _compress(kc_raw, vc_raw, bsz, t, cmp_pe, cmp_w1, cmp_w2, k_norm[0], cos_t, sin_t)
    n_sel = min(N_SEL, t // SLC_BLOCK)
    return _nsa_attn(q, cmp[0], cmp[1], ks, vs, kw, vw, gate.reshape(bsz, t, LANES), n_sel)


def kernel(x, p, positions, g_mix, g_ffn, g_ple, conv_w_pw1, conv_b_pw1, conv_w_dw, conv_b_dw, conv_ln_g, conv_ln_b, conv_w_pw2, conv_b_pw2, nsa_w_in, nsa_q_norm, nsa_k_norm, nsa_cmp_pe, nsa_cmp_w1, nsa_cmp_w2, nsa_w_out, moe_w_rg, moe_b_rg, moe_w_re, moe_b_re, moe_w_gate, moe_w_up, moe_w_down, ple_w_proj, ple_w_gate):
    bsz, t, d = x.shape
    n = bsz * t
    depth = p.shape[0]
    h = x
    p3 = p.reshape(depth, n, p.shape[-1])
    for i in range(depth):
        j = i // 2
        route_w = (g_ffn[i], moe_w_rg[i], moe_b_rg[i], moe_w_re[i], moe_b_re[i])
        if i % 2 == 0:
            u = _pw1_glu(h.reshape(n, d), g_mix[i], conv_w_pw1[j], conv_b_pw1[j])
            h, xn, slab, slab_t, counts = _conv_mix(u.reshape(bsz, t, d), h, conv_w_dw[j], conv_b_dw[j],
                                                    conv_ln_g[j], conv_ln_b[j], conv_w_pw2[j], conv_b_pw2[j], *route_w)
        else:
            o = _nsa_layer(h, positions, g_mix[i], nsa_w_in[j], nsa_q_norm[j], nsa_k_norm[j],
                           nsa_cmp_pe[j], nsa_cmp_w1[j], nsa_cmp_w2[j])
            h, xn, slab, slab_t, counts = _out_mix(o.reshape(n, d), nsa_w_out[j], h.reshape(n, d), *route_w)
        h = _moe_ple(h.reshape(n, d), xn, slab, slab_t, counts, i, p3, g_ple[i],
                     moe_w_gate, moe_w_up, moe_w_down, ple_w_gate[i], ple_w_proj[i]).reshape(bsz, t, d)
    return h
```

```python
import functools

import jax
import jax.numpy as jnp
from jax import lax
from jax.experimental import pallas as pl
from jax.experimental.pallas import tpu as pltpu

F32 = jnp.float32
BF16 = jnp.bfloat16
I32 = jnp.int32

LANES = 128
VMEM_LIMIT = 56 * 1024 * 1024

EPS = 1e-6
NEG = -1e30
LOG2_E = 1.4426950408889634

CONV_WIDTH = 31
HEAD_DIM = 64
N_KV = 4
CMP_BLOCK = 32
CMP_STRIDE = 16
SLC_BLOCK = 64
N_SEL = 16
WINDOW = 512
ROPE_THETA = 10000.0
N_GROUPS = 4
EXPERTS_PER_GROUP = 8
N_EXPERTS = N_GROUPS * EXPERTS_PER_GROUP

ROW_TILE = 512
CONV_CHUNK = 64
CONV_ROWS = 128
CONV_SUBLANES = 8
CONV_HALO = 32
EXPERT_ROWS = 256
MOVE_TILE = 256
Q_TILE = 256
K_TILE = 512
ROUTE_FIELDS = 8


def _params(n_grid):
    return pltpu.CompilerParams(dimension_semantics=("arbitrary",) * n_grid,
                                vmem_limit_bytes=VMEM_LIMIT)


def _dot(a, b):
    return jnp.dot(a, b, preferred_element_type=F32)


def _dot_nt(a, b):
    return lax.dot_general(a, b, (((1,), (1,)), ((), ())), preferred_element_type=F32)


def _rms(x, g):
    return x * lax.rsqrt(jnp.mean(x * x, axis=-1, keepdims=True) + EPS) * g


def _full(shape):
    n = len(shape)
    return pl.BlockSpec(shape, lambda *_: (0,) * n)


def _route(h, g_ffn, wcat_ref, br_ref, tri_ref, carry_ref):
    xn = _rms(h, g_ffn)
    hi = xn.astype(BF16)
    lo = (xn - hi.astype(F32)).astype(BF16)
    r1 = _dot(hi, wcat_ref[...])
    r2 = _dot(lo, wcat_ref[:, :LANES])
    logits = r1[:, :LANES] + r1[:, LANES:] + r2 + br_ref[...]

    lane = lax.broadcasted_iota(I32, logits.shape, 1).astype(F32)
    lg = jnp.where(lane < N_GROUPS, logits, -jnp.inf)
    m = jnp.max(lg, axis=-1, keepdims=True)
    grp = jnp.min(jnp.where(lg == m, lane, float(LANES)), axis=-1, keepdims=True)
    pg_sel = 1.0 / jnp.sum(jnp.exp(lg - m), axis=-1, keepdims=True)

    lo_e = N_GROUPS + EXPERTS_PER_GROUP * grp
    le = jnp.where((lane >= lo_e) & (lane < lo_e + EXPERTS_PER_GROUP), logits, -jnp.inf)
    v1 = jnp.max(le, axis=-1, keepdims=True)
    i1 = jnp.min(jnp.where(le == v1, lane, float(LANES)), axis=-1, keepdims=True)
    le2 = jnp.where(lane == i1, -jnp.inf, le)
    v2 = jnp.max(le2, axis=-1, keepdims=True)
    i2 = jnp.min(jnp.where(le2 == v2, lane, float(LANES)), axis=-1, keepdims=True)
    e21 = jnp.exp(v2 - v1)
    w1 = pg_sel / (1.0 + e21)
    w2 = pg_sel * e21 / (1.0 + e21)
    e1 = i1 - N_GROUPS
    e2 = i2 - N_GROUPS

    oh1 = jnp.where(lane == e1, 1.0, 0.0)
    oh2 = jnp.where(lane == e2, 1.0, 0.0)
    c1 = _dot(tri_ref[...], oh1.astype(BF16))
    c2 = _dot(tri_ref[...], oh2.astype(BF16))
    carry = carry_ref[...]
    tot1 = jnp.sum(oh1, axis=0, keepdims=True)
    tot2 = jnp.sum(oh2, axis=0, keepdims=True)
    rank1 = jnp.sum(oh1 * (c1 + carry), axis=-1, keepdims=True)
    rank2 = jnp.sum(oh2 * (c2 + carry + tot1), axis=-1, keepdims=True)
    carry_ref[...] = carry + tot1 + tot2

    slab = jnp.where(lane == 0, e1, jnp.where(lane == 1, e2, jnp.where(lane == 2, w1, jnp.where(
        lane == 3, w2, jnp.where(lane == 4, rank1, jnp.where(lane == 5, rank2, 0.0))))))
    return xn, slab


def _router_operands(w_rg, b_rg, w_re, b_re, tm):
    d = w_rg.shape[0]
    w = jnp.zeros((d, LANES), F32).at[:, :N_GROUPS].set(w_rg).at[:, N_GROUPS:N_GROUPS + N_EXPERTS].set(w_re)
    w_hi = w.astype(BF16)
    w_lo = (w - w_hi.astype(F32)).astype(BF16)
    wcat = jnp.concatenate([w_hi, w_lo], axis=1)
    br = jnp.zeros((1, LANES), F32).at[0, :N_GROUPS].set(b_rg).at[0, N_GROUPS:N_GROUPS + N_EXPERTS].set(b_re)
    r = lax.broadcasted_iota(I32, (tm, tm), 0)
    c = lax.broadcasted_iota(I32, (tm, tm), 1)
    tri = (c < r).astype(BF16)
    return wcat, br, tri


def _pw1_glu_kernel(x_ref, g_ref, w_ref, b_ref, o_ref, *, chunk):
    xn = _rms(x_ref[...], g_ref[...]).astype(BF16)
    d = o_ref.shape[1]
    for j in range(d // chunk):
        sa = slice(j * chunk, (j + 1) * chunk)
        sg = slice(d + j * chunk, d + (j + 1) * chunk)
        a = _dot(xn, w_ref[:, sa]) + b_ref[:, sa]
        g = _dot(xn, w_ref[:, sg]) + b_ref[:, sg]
        o_ref[:, sa] = a * jax.nn.sigmoid(g)


def _pw1_glu(x2, g, w, b):
    n, d = x2.shape
    tm = ROW_TILE
    return pl.pallas_call(
        functools.partial(_pw1_glu_kernel, chunk=512),
        grid=(n // tm,),
        in_specs=[pl.BlockSpec((tm, d), lambda i: (i, 0)), _full((1, d)), _full((d, 2 * d)), _full((1, 2 * d))],
        out_specs=pl.BlockSpec((tm, d), lambda i: (i, 0)),
        out_shape=jax.ShapeDtypeStruct((n, d), F32),
        compiler_params=_params(1),
        name="pw1_glu",
    )(x2, g.reshape(1, d), w.astype(BF16), b.reshape(1, 2 * d))


def _conv_mix_kernel(u_ref, halo_ref, x_ref, wdw_ref, bdw_ref, lng_ref, lnb_ref, w2_ref, b2_ref,
                     gffn_ref, wcat_ref, br_ref, tri_ref,
                     h_ref, xn_ref, slab_ref, slabt_ref, cnt_ref,
                     ext_ref, conv_ref, act_ref, carry_ref):
    b = pl.program_id(0)
    i = pl.program_id(1)
    tt = u_ref.shape[1]

    @pl.when((b == 0) & (i == 0))
    def _():
        carry_ref[...] = jnp.zeros_like(carry_ref)

    halo = halo_ref[0]
    ext_ref[0:CONV_HALO, :] = jnp.where(i == 0, jnp.zeros_like(halo), halo)
    ext_ref[CONV_HALO:CONV_HALO + tt, :] = u_ref[0]
    ext_ref[CONV_HALO + tt:, :] = jnp.zeros((CONV_SUBLANES, ext_ref.shape[1]), F32)
    lead = CONV_HALO - (CONV_WIDTH - 1)

    sub = CONV_SUBLANES
    for lt in range(ext_ref.shape[1] // LANES):
        cols = slice(lt * LANES, (lt + 1) * LANES)

        def conv_block(c, carry, cols=cols):
            base = pl.multiple_of(c * CONV_ROWS, CONV_ROWS)
            acc = jnp.zeros((CONV_ROWS, LANES), F32)
            for rem in range(sub):
                part = None
                for a in range(CONV_HALO // sub + 1):
                    k = rem + sub * a - lead
                    if 0 <= k < CONV_WIDTH:
                        term = wdw_ref[k:k + 1, cols] * ext_ref[pl.ds(base + sub * a, CONV_ROWS + sub), cols]
                        part = term if part is None else part + term
                acc = acc + part[rem:rem + CONV_ROWS]
            conv_ref[pl.ds(base, CONV_ROWS), cols] = acc
            return carry

        lax.fori_loop(0, tt // CONV_ROWS, conv_block, 0)

    def chunk(c, carry):
        base = pl.multiple_of(c * CONV_CHUNK, CONV_CHUNK)
        acc = conv_ref[pl.ds(base, CONV_CHUNK), :] + bdw_ref[...]
        mu = jnp.mean(acc, axis=-1, keepdims=True)
        cen = acc - mu
        var = jnp.mean(cen * cen, axis=-1, keepdims=True)
        y = cen * lax.rsqrt(var + EPS) * lng_ref[...] + lnb_ref[...]
        act_ref[pl.ds(base, CONV_CHUNK), :] = (y * jax.nn.sigmoid(y)).astype(BF16)
        return carry

    lax.fori_loop(0, tt // CONV_CHUNK, chunk, 0)

    h = x_ref[0] + _dot(act_ref[...], w2_ref[...]) + b2_ref[...]
    h_ref[0] = h
    xn, slab = _route(h, gffn_ref[...], wcat_ref, br_ref, tri_ref, carry_ref)
    xn_ref[...] = xn
    slab_ref[...] = slab
    slabt_ref[...] = slab.T[:ROUTE_FIELDS]
    cnt_ref[...] = carry_ref[...]


def _conv_mix(u, x, w_dw, b_dw, ln_g, ln_b, w_pw2, b_pw2, g_ffn, w_rg, b_rg, w_re, b_re):
    bsz, t, d = x.shape
    tt = ROW_TILE
    nt = t // tt
    n = bsz * t
    wcat, br, tri = _router_operands(w_rg, b_rg, w_re, b_re, tt)
    wdw = jnp.zeros((CONV_HALO, d), F32).at[:CONV_WIDTH].set(w_dw)
    hb = tt // CONV_HALO
    row = lambda v: v.reshape(1, d)
    return pl.pallas_call(
        _conv_mix_kernel,
        grid=(bsz, nt),
        in_specs=[
            pl.BlockSpec((1, tt, d), lambda b, i: (b, i, 0)),
            pl.BlockSpec((1, CONV_HALO, d), lambda b, i: (b, jnp.maximum(i * hb - 1, 0), 0)),
            pl.BlockSpec((1, tt, d), lambda b, i: (b, i, 0)),
            _full((CONV_HALO, d)), _full((1, d)), _full((1, d)), _full((1, d)),
            _full((d, d)), _full((1, d)), _full((1, d)),
            _full((d, 2 * LANES)), _full((1, LANES)), _full((tt, tt)),
        ],
        out_specs=[
            pl.BlockSpec((1, tt, d), lambda b, i: (b, i, 0)),
            pl.BlockSpec((tt, d), lambda b, i: (b * nt + i, 0)),
            pl.BlockSpec((tt, LANES), lambda b, i: (b * nt + i, 0)),
            pl.BlockSpec((ROUTE_FIELDS, tt), lambda b, i: (0, b * nt + i)),
            _full((1, LANES)),
        ],
        out_shape=[
            jax.ShapeDtypeStruct((bsz, t, d), F32),
            jax.ShapeDtypeStruct((n, d), F32),
            jax.ShapeDtypeStruct((n, LANES), F32),
            jax.ShapeDtypeStruct((ROUTE_FIELDS, n), F32),
            jax.ShapeDtypeStruct((1, LANES), F32),
        ],
        scratch_shapes=[pltpu.VMEM((CONV_HALO + tt + CONV_SUBLANES, d), F32), pltpu.VMEM((tt, d), F32),
                        pltpu.VMEM((tt, d), BF16),
                        pltpu.VMEM((1, LANES), F32)],
        compiler_params=_params(2),
        name="conv_mix",
    )(u, u, x, wdw, row(b_dw), row(ln_g), row(ln_b), w_pw2.astype(BF16), row(b_pw2), row(g_ffn), wcat, br, tri)


def _dispatch_kernel(dest_ref, tail_ref, nu_ref, xn_ref, xs_ref, zero_ref, sem, zsem):
    tm = xn_ref.shape[0]
    be = zero_ref.shape[0]
    base = pl.program_id(0) * tm
    n = pl.num_programs(0) * tm

    @pl.when(pl.program_id(0) == 0)
    def _():
        zero_ref[...] = jnp.zeros_like(zero_ref)

        def tail_copy(e):
            first = pl.multiple_of(jnp.maximum(tail_ref[e], 0) * be, be)
            return pltpu.make_async_copy(zero_ref, xs_ref.at[pl.ds(first, be)], zsem)

        def start(e, carry):
            @pl.when(tail_ref[e] >= 0)
            def _():
                tail_copy(e).start()
            return carry

        def finish(e, carry):
            @pl.when(tail_ref[e] >= 0)
            def _():
                tail_copy(e).wait()
            return carry

        lax.fori_loop(0, N_EXPERTS, start, 0)
        lax.fori_loop(0, N_EXPERTS, finish, 0)

        def unused_copy(blk):
            return pltpu.make_async_copy(zero_ref, xs_ref.at[pl.ds(pl.multiple_of(blk * be, be), be)], zsem)

        def start_unused(blk, carry):
            unused_copy(blk).start()
            return carry

        def finish_unused(blk, carry):
            unused_copy(blk).wait()
            return carry

        n_blocks = xs_ref.shape[0] // be
        lax.fori_loop(nu_ref[0], n_blocks, start_unused, 0)
        lax.fori_loop(nu_ref[0], n_blocks, finish_unused, 0)

    def row_copy(r, d):
        return pltpu.make_async_copy(xn_ref.at[pl.ds(r, 1)], xs_ref.at[pl.ds(d, 1)], sem)

    def issue(r, carry):
        row_copy(r, dest_ref[base + r]).start()
        row_copy(r, dest_ref[n + base + r]).start()
        return carry

    lax.fori_loop(0, tm, issue, 0, unroll=8)

    def drain(r, carry):
        row_copy(r, 0).wait()
        row_copy(r, 0).wait()
        return carry

    lax.fori_loop(0, tm, drain, 0, unroll=8)


def _dispatch(dest_flat, tail_blk, n_used, xn, n_slots):
    n, d = xn.shape
    tm = MOVE_TILE
    return pl.pallas_call(
        _dispatch_kernel,
        grid_spec=pltpu.PrefetchScalarGridSpec(
            num_scalar_prefetch=3,
            grid=(n // tm,),
            in_specs=[pl.BlockSpec((tm, d), lambda i, dest, tail, nu: (i, 0))],
            out_specs=pl.BlockSpec(memory_space=pl.ANY),
            scratch_shapes=[pltpu.VMEM((EXPERT_ROWS, d), F32), pltpu.SemaphoreType.DMA(()),
                            pltpu.SemaphoreType.DMA(())],
        ),
        out_shape=jax.ShapeDtypeStruct((n_slots, d), F32),
        compiler_params=_params(1),
        name="moe_dispatch",
    )(dest_flat, tail_blk, n_used, xn)


def _expert_kernel(be_ref, nu_ref, xs_ref, wg_ref, wu_ref, wd_ref, y_ref, wgb_ref, wub_ref, wdb_ref):
    i = pl.program_id(0)
    changed = (i == 0) | (be_ref[i] != be_ref[jnp.maximum(i - 1, 0)])

    @pl.when(changed)
    def _():
        wgb_ref[...] = wg_ref[0, 0].astype(BF16)
        wub_ref[...] = wu_ref[0, 0].astype(BF16)
        wdb_ref[...] = wd_ref[0, 0].astype(BF16)

    @pl.when(i < nu_ref[0])
    def _():
        x = xs_ref[...].astype(BF16)
        a = _dot(x, wgb_ref[...])
        u = _dot(x, wub_ref[...])
        hid = (a * jax.nn.sigmoid(a) * u).astype(BF16)
        y_ref[...] = _dot(hid, wdb_ref[...])

    @pl.when(i >= nu_ref[0])
    def _():
        y_ref[...] = jnp.zeros_like(y_ref)


def _experts(blk_e, n_used, xs, layer, w_gate, w_up, w_down):
    p_rows, d = xs.shape
    f = w_gate.shape[3]
    be = EXPERT_ROWS
    return pl.pallas_call(
        _expert_kernel,
        grid_spec=pltpu.PrefetchScalarGridSpec(
            num_scalar_prefetch=2,
            grid=(p_rows // be,),
            in_specs=[
                pl.BlockSpec((be, d), lambda i, e, nu: (jnp.minimum(i, nu[0] - 1), 0)),
                pl.BlockSpec((1, 1, d, f), lambda i, e, nu: (layer, e[i], 0, 0)),
                pl.BlockSpec((1, 1, d, f), lambda i, e, nu: (layer, e[i], 0, 0)),
                pl.BlockSpec((1, 1, f, d), lambda i, e, nu: (layer, e[i], 0, 0)),
            ],
            out_specs=pl.BlockSpec((be, d), lambda i, e, nu: (i, 0)),
            scratch_shapes=[pltpu.VMEM((d, f), BF16), pltpu.VMEM((d, f), BF16), pltpu.VMEM((f, d), BF16)],
        ),
        out_shape=jax.ShapeDtypeStruct((p_rows, d), F32),
        compiler_params=_params(1),
        name="moe_experts",
    )(blk_e, n_used, xs, w_gate, w_up, w_down)


def _combine_ple_kernel(dest_ref, h_ref, slab_ref, p_ref, gple_ref, wg_ref, wp_ref, y_hbm,
                        o_ref, ybuf, sems):
    tm = h_ref.shape[0]
    i = pl.program_id(0)
    n_steps = pl.num_programs(0)
    n = n_steps * tm
    slot = i % 2

    def row_copy(step_slot, k, r, d):
        return pltpu.make_async_copy(y_hbm.at[pl.ds(d, 1)], ybuf.at[step_slot, k, pl.ds(r, 1)],
                                     sems.at[step_slot])

    def issue(step, step_slot):
        base = step * tm

        def body(r, carry):
            row_copy(step_slot, 0, r, dest_ref[base + r]).start()
            row_copy(step_slot, 1, r, dest_ref[n + base + r]).start()
            return carry

        lax.fori_loop(0, tm, body, 0, unroll=8)

    @pl.when(i == 0)
    def _():
        issue(0, 0)

    @pl.when(i + 1 < n_steps)
    def _():
        issue(i + 1, 1 - slot)

    def drain(r, carry):
        row_copy(slot, 0, r, 0).wait()
        row_copy(slot, 1, r, 0).wait()
        return carry

    lax.fori_loop(0, tm, drain, 0, unroll=8)

    slab = slab_ref[...]
    lane = lax.broadcasted_iota(I32, slab.shape, 1)
    w0 = jnp.sum(jnp.where(lane == 2, slab, 0.0), axis=-1, keepdims=True)
    w1 = jnp.sum(jnp.where(lane == 3, slab, 0.0), axis=-1, keepdims=True)
    hm = h_ref[...] + w0 * ybuf[slot, 0] + w1 * ybuf[slot, 1]
    gate = jax.nn.sigmoid(_dot(_rms(hm, gple_ref[...]).astype(BF16), wg_ref[...]))
    o_ref[...] = hm + gate * _dot(p_ref[0].astype(BF16), wp_ref[...])


def _combine_ple(dest_flat, h2, slab, layer, p3, g_ple, w_gate, w_proj, y):
    n, d = h2.shape
    dp = p3.shape[2]
    tm = MOVE_TILE
    return pl.pallas_call(
        _combine_ple_kernel,
        grid_spec=pltpu.PrefetchScalarGridSpec(
            num_scalar_prefetch=1,
            grid=(n // tm,),
            in_specs=[
                pl.BlockSpec((tm, d), lambda i, dest: (i, 0)),
                pl.BlockSpec((tm, LANES), lambda i, dest: (i, 0)),
                pl.BlockSpec((1, tm, dp), lambda i, dest: (layer, i, 0)),
                pl.BlockSpec((1, d), lambda i, dest: (0, 0)),
                pl.BlockSpec((d, d), lambda i, dest: (0, 0)),
                pl.BlockSpec((dp, d), lambda i, dest: (0, 0)),
                pl.BlockSpec(memory_space=pl.ANY),
            ],
            out_specs=pl.BlockSpec((tm, d), lambda i, dest: (i, 0)),
            scratch_shapes=[pltpu.VMEM((2, 2, tm, d), F32), pltpu.SemaphoreType.DMA((2,))],
        ),
        out_shape=jax.ShapeDtypeStruct((n, d), F32),
        compiler_params=_params(1),
        name="moe_combine_ple",
    )(dest_flat, h2, slab, p3, g_ple.reshape(1, d), w_gate.astype(BF16), w_proj.astype(BF16), y)


def _moe_ple(h2, xn, slab, slab_t, counts, layer, p3, g_ple, w_gate_e, w_up_e, w_down_e, ple_w_gate, ple_w_proj):
    n, d = h2.shape
    be = EXPERT_ROWS
    n_blocks = (2 * n) // be + N_EXPERTS
    cnt = counts[0, :N_EXPERTS].astype(I32)
    padded = (cnt + be - 1) // be * be
    ends = jnp.cumsum(padded)
    starts = ends - padded
    eid = slab_t[0:2].astype(I32)
    rank = slab_t[4:6].astype(I32)
    expert_ids = jnp.arange(N_EXPERTS, dtype=I32).reshape(N_EXPERTS, 1, 1)
    slot_base = jnp.sum(jnp.where(eid[None] == expert_ids, starts.reshape(N_EXPERTS, 1, 1), 0), axis=0)
    dest_flat = (slot_base + rank).reshape(2 * n)
    block_start = jnp.arange(n_blocks, dtype=I32) * be
    blk_e = jnp.minimum(jnp.sum((ends[None, :] <= block_start[:, None]).astype(I32), axis=1), N_EXPERTS - 1)
    n_used = (ends[-1:] // be).astype(I32)
    tail_blk = jnp.where(padded > 0, ends // be - 1, -1).astype(I32)
    xs = _dispatch(dest_flat, tail_blk, n_used, xn, n_blocks * be)
    y = _experts(blk_e, n_used, xs, layer, w_gate_e, w_up_e, w_down_e)
    return _combine_ple(dest_flat, h2, slab, layer, p3, g_ple, ple_w_gate, ple_w_proj, y)


def _rope_table_kernel(pos_ref, inv_ref, sign_ref, cos_ref, sin_ref):
    ang = pos_ref[...] * inv_ref[...]
    cos_ref[...] = jnp.cos(ang)
    sin_ref[...] = jnp.sin(ang) * sign_ref[...]


def _rope_table(positions):
    n = positions.size
    tm = ROW_TILE
    half = HEAD_DIM // 2
    lane = jnp.arange(LANES)
    inv = 1.0 / (ROPE_THETA ** ((lane % half).astype(F32) / half))
    sign = jnp.where(lane % HEAD_DIM < half, -1.0, 1.0).astype(F32)
    return pl.pallas_call(
        _rope_table_kernel,
        grid=(n // tm,),
        in_specs=[pl.BlockSpec((tm, 1), lambda i: (i, 0)), _full((1, LANES)), _full((1, LANES))],
        out_specs=[pl.BlockSpec((tm, LANES), lambda i: (i, 0))] * 2,
        out_shape=[jax.ShapeDtypeStruct((n, LANES), F32)] * 2,
        compiler_params=_params(1),
        name="rope_table",
    )(positions.astype(F32).reshape(n, 1), inv.reshape(1, LANES), sign.reshape(1, LANES))


def _head_norm_rope(z, gain, seg_ref, cos, sin):
    z2 = z * z
    hi = z2.astype(BF16)
    lo = (z2 - hi.astype(F32)).astype(BF16)
    ssq = _dot(hi, seg_ref[...]) + _dot(lo, seg_ref[...])
    zn = z * lax.rsqrt(ssq * (1.0 / HEAD_DIM) + EPS) * gain
    width = z.shape[1]
    half = HEAD_DIM // 2
    lane = lax.broadcasted_iota(I32, z.shape, 1)
    rot = jnp.where((lane & half) == 0, pltpu.roll(zn, width - half, 1), pltpu.roll(zn, half, 1))
    return zn * cos + rot * sin


def _nsa_proj_kernel(h_ref, g_ref, w_ref, cos_ref, sin_ref, qn_ref, kns_ref, knw_ref, seg_ref,
                     q_ref, kc_ref, vc_ref, ks_ref, vs_ref, kw_ref, vw_ref, gate_ref):
    d = h_ref.shape[1]
    kd = N_KV * HEAD_DIM
    xn = _rms(h_ref[...], g_ref[...]).astype(BF16)
    cos = jnp.concatenate([cos_ref[...], cos_ref[...]], axis=1)
    sin = jnp.concatenate([sin_ref[...], sin_ref[...]], axis=1)
    scale = HEAD_DIM ** -0.5 * LOG2_E

    def heads_out(ref, first, z):
        ones = jnp.ones((z.shape[0], HEAD_DIM), ref.dtype)
        for r in range(N_KV):
            zr = z[:, r * HEAD_DIM:(r + 1) * HEAD_DIM].astype(ref.dtype)
            ref[0, first + r] = zr if ref.shape[-1] == HEAD_DIM else jnp.concatenate([zr, ones], axis=1)

    for j in range(d // kd):
        z = _dot(xn, w_ref[:, j * kd:(j + 1) * kd])
        heads_out(q_ref, N_KV * j, _head_norm_rope(z, qn_ref[...], seg_ref, cos, sin) * scale)
    col = lambda c: _dot(xn, w_ref[:, d + c * kd:d + (c + 1) * kd])
    heads_out(kc_ref, 0, col(0))
    heads_out(vc_ref, 0, col(1))
    heads_out(ks_ref, 0, _head_norm_rope(col(2), kns_ref[...], seg_ref, cos, sin))
    heads_out(vs_ref, 0, col(3))
    heads_out(kw_ref, 0, _head_norm_rope(col(4), knw_ref[...], seg_ref, cos, sin))
    heads_out(vw_ref, 0, col(5))
    gate_ref[...] = jax.nn.sigmoid(_dot(xn, w_ref[:, d + 6 * kd:]))


def _nsa_proj(h, g_mix, w_in, cos_t, sin_t, q_norm, k_norm):
    bsz, t, d = h.shape
    n = bsz * t
    tm = ROW_TILE
    nt = t // tm
    kd = N_KV * HEAD_DIM
    n_heads = d // HEAD_DIM
    n_in = w_in.shape[1]
    w_pad = jnp.zeros((d, d + 6 * kd + LANES), F32).at[:, :n_in].set(w_in).astype(BF16)
    tile4 = lambda v: jnp.tile(v, kd // HEAD_DIM).reshape(1, kd)
    idx = jnp.arange(kd) // HEAD_DIM
    seg = (idx[:, None] == idx[None, :]).astype(BF16)
    head_spec = lambda nh, w=HEAD_DIM: pl.BlockSpec((1, nh, tm, w), lambda i: (i // nt, 0, i % nt, 0))
    head_shape = lambda nh, w=HEAD_DIM, dt=BF16: jax.ShapeDtypeStruct((bsz, nh, t, w), dt)
    flat_spec = lambda w: pl.BlockSpec((tm, w), lambda i: (i, 0))
    vw2 = 2 * HEAD_DIM
    return pl.pallas_call(
        _nsa_proj_kernel,
        grid=(n // tm,),
        in_specs=[flat_spec(d), _full((1, d)), _full(w_pad.shape), flat_spec(LANES), flat_spec(LANES),
                  _full((1, kd)), _full((1, kd)), _full((1, kd)), _full((kd, kd))],
        out_specs=[head_spec(n_heads), head_spec(N_KV), head_spec(N_KV), head_spec(N_KV), head_spec(N_KV, vw2),
                   head_spec(N_KV), head_spec(N_KV, vw2), flat_spec(LANES)],
        out_shape=[head_shape(n_heads), head_shape(N_KV, dt=F32), head_shape(N_KV, dt=F32),
                   head_shape(N_KV), head_shape(N_KV, vw2), head_shape(N_KV), head_shape(N_KV, vw2),
                   jax.ShapeDtypeStruct((n, LANES), F32)],
        compiler_params=_params(1),
        name="nsa_proj",
    )(h.reshape(n, d), g_mix.reshape(1, d), w_pad, cos_t, sin_t,
      tile4(q_norm), tile4(k_norm[1]), tile4(k_norm[2]), seg)


def _compress_kernel(xk_ref, xv_ref, pe_ref, w1_ref, w2_ref, kn_ref, cos_ref, sin_ref, o_ref):
    nch, dh = o_ref.shape[3], o_ref.shape[4]
    st = CMP_STRIDE
    hidden = w1_ref.shape[2]
    for which, x_ref in enumerate((xk_ref, xv_ref)):
        first = jnp.zeros((nch, hidden), F32)
        second = jnp.zeros((nch, hidden), F32)
        for tp in range(st):
            xt = x_ref[0, 0, pl.ds(tp, nch, stride=st), :]
            first = first + _dot((xt + pe_ref[which, tp:tp + 1, :]).astype(BF16),
                                 w1_ref[which, tp * dh:(tp + 1) * dh, :])
            second = second + _dot((xt + pe_ref[which, st + tp:st + tp + 1, :]).astype(BF16),
                                   w1_ref[which, (st + tp) * dh:(st + tp + 1) * dh, :])
        hid = first + pltpu.roll(second, nch - 1, 0)
        c = _dot(jax.nn.gelu(hid, approximate=True).astype(BF16), w2_ref[which])
        if which == 0:
            half = dh // 2
            cn = _rms(c, kn_ref[...])
            rot = jnp.concatenate([cn[:, half:], cn[:, :half]], axis=1)
            c = cn * cos_ref[0] + rot * sin_ref[0]
        o_ref[which, 0, 0] = c.astype(o_ref.dtype)


def _compress(kc_raw, vc_raw, pe, w1, w2, k_norm0, cos_t, sin_t):
    bsz, g, t, dh = kc_raw.shape
    st = CMP_STRIDE
    nch = t // st
    last = lambda tab: jnp.pad(tab.reshape(bsz, t, LANES)[:, CMP_BLOCK - 1::st, :dh], ((0, 0), (0, 1), (0, 0)))
    src = pl.BlockSpec((1, 1, t, dh), lambda b, gi: (b, gi, 0, 0))
    tab = pl.BlockSpec((1, nch, dh), lambda b, gi: (b, 0, 0))
    return pl.pallas_call(
        _compress_kernel,
        grid=(bsz, g),
        in_specs=[src, src, _full(pe.shape), _full(w1.shape), _full(w2.shape), _full((1, dh)), tab, tab],
        out_specs=pl.BlockSpec((2, 1, 1, nch, dh), lambda b, gi: (0, b, gi, 0, 0)),
        out_shape=jax.ShapeDtypeStruct((2, bsz, g, nch, dh), BF16),
        compiler_params=_params(2),
        name="nsa_compress",
    )(kc_raw, vc_raw, pe, w1.astype(BF16), w2.astype(BF16), k_norm0.reshape(1, dh), last(cos_t), last(sin_t))


def _attend_chunk(q, k, v, bias, m_ref, acc_ref, r_heads, tq_n):
    tk = k.shape[0]
    s = _dot_nt(q, k)
    m_all = m_ref[...]
    acc_all = acc_ref[...]
    m_out, scaled, probs = [], [], []
    for r in range(r_heads):
        rs = slice(r * tq_n, (r + 1) * tq_n)
        sr = s[rs] + bias
        m_old = m_all[rs]
        m_new = jnp.maximum(m_old, jnp.max(sr, axis=-1, keepdims=True))
        m_wide = m_new if tk == m_new.shape[1] else jnp.concatenate([m_new] * (tk // m_new.shape[1]), axis=1)
        probs.append(jnp.exp2(sr - m_wide).astype(BF16))
        m_out.append(m_new)
        scaled.append(jnp.exp2(m_old - m_new) * acc_all[rs])
    m_ref[...] = jnp.concatenate(m_out, axis=0)
    acc_ref[...] = jnp.concatenate(scaled, axis=0) + _dot(jnp.concatenate(probs, axis=0), v)


def _nsa_attn_kernel(q_ref, kc_ref, vc_ref, ks_ref, vs_ref, kw_ref, vw_ref, gate_ref, wsel_ref, e_ref, o_ref,
                     m_ref, acc_ref, og_ref, bias_ref, *, n_sel):
    g = pl.program_id(1)
    qi = pl.program_id(2)
    r_heads, tq_n, dh = q_ref.shape[1], q_ref.shape[2], q_ref.shape[3]
    rows = r_heads * tq_n
    t0 = qi * tq_n
    tk = K_TILE
    q = q_ref[0].reshape(rows, dh)
    head_rows = [slice(r * tq_n, (r + 1) * tq_n) for r in range(r_heads)]

    gates = gate_ref[0]
    gate_lane = lax.broadcasted_iota(I32, gates.shape, 1)

    def gate(r, branch):
        col = (g * r_heads + r) * 3 + branch
        return jnp.sum(jnp.where(gate_lane == col, gates, 0.0), axis=-1, keepdims=True)

    def restart():
        m_ref[...] = jnp.full(m_ref.shape, NEG, F32)
        acc_ref[...] = jnp.zeros(acc_ref.shape, F32)

    def add_branch(branch):
        acc = acc_ref[...]
        o = acc * (1.0 / pltpu.roll(acc, dh, 1))
        og_ref[...] += jnp.concatenate([gate(r, branch) * o[rs, :dh] for r, rs in enumerate(head_rows)], axis=0)

    kc = kc_ref[0, 0]
    nc = kc.shape[0]
    tq_c = t0 + lax.broadcasted_iota(I32, (tq_n, nc), 0)
    mask_c = CMP_STRIDE * lax.broadcasted_iota(I32, (tq_n, nc), 1) + (CMP_BLOCK - 1) <= tq_c
    s_c = _dot_nt(q, kc)
    imp = jnp.zeros((tq_n, nc), F32)
    probs = []
    for rs in head_rows:
        sr = jnp.where(mask_c, s_c[rs], NEG)
        e = jnp.where(mask_c, jnp.exp2(sr - jnp.max(sr, axis=-1, keepdims=True)), 0.0)
        l = jnp.sum(e, axis=-1, keepdims=True)
        p = e * jnp.where(l > 0.0, 1.0 / l, 0.0)
        imp = imp + p
        probs.append(p.astype(BF16))
    o_c = _dot(jnp.concatenate(probs, axis=0), vc_ref[0, 0])
    og_ref[...] = jnp.concatenate([gate(r, 0) * o_c[rs] for r, rs in enumerate(head_rows)], axis=0)

    imp_hi = imp.astype(BF16)
    imp_lo = (imp - imp_hi.astype(F32)).astype(BF16)
    p_slc = _dot_nt(wsel_ref[...], imp_hi) + _dot_nt(wsel_ref[...], imp_lo)
    nsb = p_slc.shape[0]
    blk = lax.broadcasted_iota(I32, (nsb, tq_n), 0)
    tq_t = t0 + lax.broadcasted_iota(I32, (nsb, tq_n), 1)
    cur = tq_t // SLC_BLOCK
    forced = (blk == 0) | (blk == cur) | (blk == cur - 1)
    score = jnp.where(forced, jnp.inf, jnp.where(blk * SLC_BLOCK <= tq_t, p_slc, -jnp.inf))
    cnt = jnp.zeros((nsb, tq_n), F32)
    for i in range(nsb):
        ri = score[i:i + 1, :]
        beats = (ri > score) | ((ri == score) & (blk > i))
        cnt = cnt + jnp.where(beats, 1.0, 0.0)
    dropped = jnp.where(cnt < n_sel, 0.0, 1.0)
    dropped = jnp.concatenate([dropped, jnp.zeros((LANES - nsb, tq_n), F32)], axis=0).T.astype(BF16)

    kwn = WINDOW + tq_n
    start = pl.multiple_of(jnp.maximum(t0 - WINDOW, 0), tq_n)
    kpos_w = start + lax.broadcasted_iota(I32, (tq_n, kwn), 1)
    tq_w = t0 + lax.broadcasted_iota(I32, (tq_n, kwn), 0)
    bias_w = jnp.where((kpos_w <= tq_w) & (kpos_w > tq_w - WINDOW), 0.0, NEG)
    s_w = _dot_nt(q, kw_ref[0, 0, pl.ds(start, kwn), :])
    probs = []
    for rs in head_rows:
        sr = s_w[rs] + bias_w
        probs.append(jnp.exp2(sr - jnp.max(sr, axis=-1, keepdims=True)).astype(BF16))
    acc_w = _dot(jnp.concatenate(probs, axis=0), vw_ref[0, 0, pl.ds(start, kwn), :])
    o_w = acc_w * (1.0 / pltpu.roll(acc_w, dh, 1))
    og_ref[...] += jnp.concatenate([gate(r, 2) * o_w[rs, :dh] for r, rs in enumerate(head_rows)], axis=0)

    t_keys = e_ref.shape[1]
    kpos = lax.broadcasted_iota(I32, (tq_n, t_keys), 1)
    tq_s = t0 + lax.broadcasted_iota(I32, (tq_n, t_keys), 0)
    bias_all = jnp.where(kpos <= tq_s, _dot(dropped, e_ref[...]), NEG)
    for c in range(t_keys // tk):
        bias_ref[c] = bias_all[:, c * tk:(c + 1) * tk]
    restart()

    def sel_step(c, carry):
        k0 = pl.multiple_of(c * tk, tk)
        _attend_chunk(q, ks_ref[0, 0, pl.ds(k0, tk), :], vs_ref[0, 0, pl.ds(k0, tk), :], bias_ref[c],
                      m_ref, acc_ref, r_heads, tq_n)
        return carry

    lax.fori_loop(0, (t0 + tq_n + tk - 1) // tk, sel_step, 0)
    add_branch(1)

    for r, rs in enumerate(head_rows):
        o_ref[0, :, r * dh:(r + 1) * dh] = og_ref[rs].astype(o_ref.dtype)


def _nsa_attn(q, kc, vc, ks, vs, kw, vw, gate, n_sel):
    bsz, n_heads, t, dh = q.shape
    g = N_KV
    r_heads = n_heads // g
    tq = Q_TILE
    nch = kc.shape[2]
    a = SLC_BLOCK // CMP_STRIDE
    bb = CMP_BLOCK // CMP_STRIDE
    nsb = t // SLC_BLOCK
    j = jnp.arange(nsb)[:, None]
    c = jnp.arange(nch)[None, :]
    wsel = sum(((c == a * j + m + n_) & (c < nch - 1)).astype(F32)
               for m in range(a) for n_ in range(bb)).astype(BF16)
    tk = K_TILE
    key_blk = (jnp.arange(t) // SLC_BLOCK).reshape(1, t)
    e = jnp.where(key_blk == jnp.arange(LANES).reshape(LANES, 1), NEG, 0.0).astype(BF16)
    rows = r_heads * tq
    kv_spec = lambda n_rows, w=dh: pl.BlockSpec((1, 1, n_rows, w), lambda b, gi, qi: (b, gi, 0, 0))
    return pl.pallas_call(
        functools.partial(_nsa_attn_kernel, n_sel=n_sel),
        grid=(bsz, g, t // tq),
        in_specs=[
            pl.BlockSpec((1, r_heads, tq, dh), lambda b, gi, qi: (b, gi, qi, 0)),
            kv_spec(nch), kv_spec(nch), kv_spec(t), kv_spec(t, 2 * dh), kv_spec(t), kv_spec(t, 2 * dh),
            pl.BlockSpec((1, tq, LANES), lambda b, gi, qi: (b, qi, 0)),
            pl.BlockSpec((nsb, nch), lambda b, gi, qi: (0, 0)),
            pl.BlockSpec(e.shape, lambda b, gi, qi: (0, 0)),
        ],
        out_specs=pl.BlockSpec((1, tq, r_heads * dh), lambda b, gi, qi: (b, qi, gi)),
        out_shape=jax.ShapeDtypeStruct((bsz, t, n_heads * dh), BF16),
        scratch_shapes=[pltpu.VMEM((rows, 2 * dh), F32), pltpu.VMEM((rows, 2 * dh), F32),
                        pltpu.VMEM((rows, dh), F32), pltpu.VMEM((t // tk, tq, tk), F32)],
        compiler_params=_params(3),
        name="nsa_attention",
    )(q, kc, vc, ks, vs, kw, vw, gate, wsel, e)


def _out_mix_kernel(o_ref, w_ref, h_ref, gffn_ref, wcat_ref, br_ref, tri_ref,
                    h2_ref, xn_ref, slab_ref, slabt_ref, cnt_ref, carry_ref):
    @pl.when(pl.program_id(0) == 0)
    def _():
        carry_ref[...] = jnp.zeros_like(carry_ref)

    h = h_ref[...] + _dot(o_ref[...], w_ref[...])
    h2_ref[...] = h
    xn, slab = _route(h, gffn_ref[...], wcat_ref, br_ref, tri_ref, carry_ref)
    xn_ref[...] = xn
    slab_ref[...] = slab
    slabt_ref[...] = slab.T[:ROUTE_FIELDS]
    cnt_ref[...] = carry_ref[...]


def _out_mix(o2, w_out, h2, g_ffn, w_rg, b_rg, w_re, b_re):
    n, d = h2.shape
    tm = ROW_TILE
    wcat, br, tri = _router_operands(w_rg, b_rg, w_re, b_re, tm)
    row = pl.BlockSpec((tm, d), lambda i: (i, 0))
    return pl.pallas_call(
        _out_mix_kernel,
        grid=(n // tm,),
        in_specs=[row, _full((d, d)), row, _full((1, d)), _full((d, 2 * LANES)), _full((1, LANES)), _full((tm, tm))],
        out_specs=[row, row, pl.BlockSpec((tm, LANES), lambda i: (i, 0)),
                   pl.BlockSpec((ROUTE_FIELDS, tm), lambda i: (0, i)), _full((1, LANES))],
        out_shape=[jax.ShapeDtypeStruct((n, d), F32), jax.ShapeDtypeStruct((n, d), F32),
                   jax.ShapeDtypeStruct((n, LANES), F32), jax.ShapeDtypeStruct((ROUTE_FIELDS, n), F32),
                   jax.ShapeDtypeStruct((1, LANES), F32)],
        scratch_shapes=[pltpu.VMEM((1, LANES), F32)],
        compiler_params=_params(1),
        name="nsa_out_mix",
    )(o2, w_out.astype(BF16), h2, g_ffn.reshape(1, d), wcat, br, tri)


def _nsa_layer(h, positions, g_mix, w_in, q_norm, k_norm, cmp_pe, cmp_w1, cmp_w2):
    bsz, t, d = h.shape
    cos_t, sin_t = _rope_table(positions)
    q, kc_raw, vc_raw, ks, vs, kw, vw, gate = _nsa_proj(h, g_mix, w_in, cos_t, sin_t, q_norm, k_norm)
    cmp = _compress(kc_raw, vc_raw, cmp_pe, cmp_w1, cmp_w2, k_norm[0], cos_t, sin_t)
    n_sel = min(N_SEL, t // SLC_BLOCK)
    return _nsa_attn(q, cmp[0], cmp[1], ks, vs, kw, vw, gate.reshape(bsz, t, LANES), n_sel)


def kernel(x, p, positions, g_mix, g_ffn, g_ple, conv_w_pw1, conv_b_pw1, conv_w_dw, conv_b_dw, conv_ln_g, conv_ln_b, conv_w_pw2, conv_b_pw2, nsa_w_in, nsa_q_norm, nsa_k_norm, nsa_cmp_pe, nsa_cmp_w1, nsa_cmp_w2, nsa_w_out, moe_w_rg, moe_b_rg, moe_w_re, moe_b_re, moe_w_gate, moe_w_up, moe_w_down, ple_w_proj, ple_w_gate):
    bsz, t, d = x.shape
    n = bsz * t
    depth = p.shape[0]
    h = x
    p3 = p.reshape(depth, n, p.shape[-1])
    for i in range(depth):
        j = i // 2
        route_w = (g_ffn[i], moe_w_rg[i], moe_b_rg[i], moe_w_re[i], moe_b_re[i])
        if i % 2 == 0:
            u = _pw1_glu(h.reshape(n, d), g_mix[i], conv_w_pw1[j], conv_b_pw1[j])
            h, xn, slab, slab_t, counts = _conv_mix(u.reshape(bsz, t, d), h, conv_w_dw[j], conv_b_dw[j],
                                                    conv_ln_g[j], conv_ln_b[j], conv_w_pw2[j], conv_b_pw2[j], *route_w)
        else:
            o = _nsa_layer(h, positions, g_mix[i], nsa_w_in[j], nsa_q_norm[j], nsa_k_norm[j],
                           nsa_cmp_pe[j], nsa_cmp_w1[j], nsa_cmp_w2[j])
            h, xn, slab, slab_t, counts = _out_mix(o.reshape(n, d), nsa_w_out[j], h.reshape(n, d), *route_w)
        h = _moe_ple(h.reshape(n, d), xn, slab, slab_t, counts, i, p3, g_ple[i],
                     moe_w_gate, moe_w_up, moe_w_down, ple_w_gate[i], ple_w_proj[i]).reshape(bsz, t, d)
    return h
```

```python
import functools

import jax
import jax.numpy as jnp
from jax import lax
from jax.experimental import pallas as pl
from jax.experimental.pallas import tpu as pltpu

F32 = jnp.float32
BF16 = jnp.bfloat16
I32 = jnp.int32

LANES = 128
VMEM_LIMIT = 56 * 1024 * 1024

EPS = 1e-6
NEG = -1e30
LOG2_E = 1.4426950408889634

CONV_WIDTH = 31
HEAD_DIM = 64
N_KV = 4
CMP_BLOCK = 32
CMP_STRIDE = 16
SLC_BLOCK = 64
N_SEL = 16
WINDOW = 512
ROPE_THETA = 10000.0
N_GROUPS = 4
EXPERTS_PER_GROUP = 8
N_EXPERTS = N_GROUPS * EXPERTS_PER_GROUP

ROW_TILE = 512
CONV_CHUNK = 64
CONV_ROWS = 128
CONV_SUBLANES = 8
CONV_HALO = 32
EXPERT_ROWS = 256
Q_TILE = 256
K_TILE = 512
ROUTE_FIELDS = 8


def _params(n_grid):
    return pltpu.CompilerParams(dimension_semantics=("arbitrary",) * n_grid,
                                vmem_limit_bytes=VMEM_LIMIT)


def _dot(a, b):
    return jnp.dot(a, b, preferred_element_type=F32)


def _dot_nt(a, b):
    return lax.dot_general(a, b, (((1,), (1,)), ((), ())), preferred_element_type=F32)


def _rms(x, g):
    return x * lax.rsqrt(jnp.mean(x * x, axis=-1, keepdims=True) + EPS) * g


def _full(shape):
    n = len(shape)
    return pl.BlockSpec(shape, lambda *_: (0,) * n)


def _route(h, g_ffn, wcat_ref, br_ref, tri_ref, carry_ref):
    xn = _rms(h, g_ffn)
    hi = xn.astype(BF16)
    lo = (xn - hi.astype(F32)).astype(BF16)
    r1 = _dot(hi, wcat_ref[...])
    r2 = _dot(lo, wcat_ref[:, :LANES])
    logits = r1[:, :LANES] + r1[:, LANES:] + r2 + br_ref[...]

    lane = lax.broadcasted_iota(I32, logits.shape, 1).astype(F32)
    lg = jnp.where(lane < N_GROUPS, logits, -jnp.inf)
    m = jnp.max(lg, axis=-1, keepdims=True)
    grp = jnp.min(jnp.where(lg == m, lane, float(LANES)), axis=-1, keepdims=True)
    pg_sel = 1.0 / jnp.sum(jnp.exp(lg - m), axis=-1, keepdims=True)

    lo_e = N_GROUPS + EXPERTS_PER_GROUP * grp
    le = jnp.where((lane >= lo_e) & (lane < lo_e + EXPERTS_PER_GROUP), logits, -jnp.inf)
    v1 = jnp.max(le, axis=-1, keepdims=True)
    i1 = jnp.min(jnp.where(le == v1, lane, float(LANES)), axis=-1, keepdims=True)
    le2 = jnp.where(lane == i1, -jnp.inf, le)
    v2 = jnp.max(le2, axis=-1, keepdims=True)
    i2 = jnp.min(jnp.where(le2 == v2, lane, float(LANES)), axis=-1, keepdims=True)
    e21 = jnp.exp(v2 - v1)
    w1 = pg_sel / (1.0 + e21)
    w2 = pg_sel * e21 / (1.0 + e21)
    e1 = i1 - N_GROUPS
    e2 = i2 - N_GROUPS

    oh1 = jnp.where(lane == e1, 1.0, 0.0)
    oh2 = jnp.where(lane == e2, 1.0, 0.0)
    c1 = _dot(tri_ref[...], oh1.astype(BF16))
    c2 = _dot(tri_ref[...], oh2.astype(BF16))
    carry = carry_ref[...]
    tot1 = jnp.sum(oh1, axis=0, keepdims=True)
    tot2 = jnp.sum(oh2, axis=0, keepdims=True)
    rank1 = jnp.sum(oh1 * (c1 + carry), axis=-1, keepdims=True)
    rank2 = jnp.sum(oh2 * (c2 + carry + tot1), axis=-1, keepdims=True)
    carry_ref[...] = carry + tot1 + tot2

    slab = jnp.where(lane == 0, e1, jnp.where(lane == 1, e2, jnp.where(lane == 2, w1, jnp.where(
        lane == 3, w2, jnp.where(lane == 4, rank1, jnp.where(lane == 5, rank2, 0.0))))))
    return xn, slab


def _router_operands(w_rg, b_rg, w_re, b_re, tm):
    d = w_rg.shape[0]
    w = jnp.zeros((d, LANES), F32).at[:, :N_GROUPS].set(w_rg).at[:, N_GROUPS:N_GROUPS + N_EXPERTS].set(w_re)
    w_hi = w.astype(BF16)
    w_lo = (w - w_hi.astype(F32)).astype(BF16)
    wcat = jnp.concatenate([w_hi, w_lo], axis=1)
    br = jnp.zeros((1, LANES), F32).at[0, :N_GROUPS].set(b_rg).at[0, N_GROUPS:N_GROUPS + N_EXPERTS].set(b_re)
    r = lax.broadcasted_iota(I32, (tm, tm), 0)
    c = lax.broadcasted_iota(I32, (tm, tm), 1)
    tri = (c < r).astype(BF16)
    return wcat, br, tri


def _pw1_glu_kernel(x_ref, g_ref, w_ref, b_ref, o_ref, *, chunk):
    xn = _rms(x_ref[...], g_ref[...]).astype(BF16)
    d = o_ref.shape[1]
    for j in range(d // chunk):
        sa = slice(j * chunk, (j + 1) * chunk)
        sg = slice(d + j * chunk, d + (j + 1) * chunk)
        a = _dot(xn, w_ref[:, sa]) + b_ref[:, sa]
        g = _dot(xn, w_ref[:, sg]) + b_ref[:, sg]
        o_ref[:, sa] = a * jax.nn.sigmoid(g)


def _pw1_glu(x2, g, w, b):
    n, d = x2.shape
    tm = ROW_TILE
    return pl.pallas_call(
        functools.partial(_pw1_glu_kernel, chunk=512),
        grid=(n // tm,),
        in_specs=[pl.BlockSpec((tm, d), lambda i: (i, 0)), _full((1, d)), _full((d, 2 * d)), _full((1, 2 * d))],
        out_specs=pl.BlockSpec((tm, d), lambda i: (i, 0)),
        out_shape=jax.ShapeDtypeStruct((n, d), F32),
        compiler_params=_params(1),
        name="pw1_glu",
    )(x2, g.reshape(1, d), w.astype(BF16), b.reshape(1, 2 * d))


def _conv_mix_kernel(u_ref, halo_ref, x_ref, wdw_ref, bdw_ref, lng_ref, lnb_ref, w2_ref, b2_ref,
                     gffn_ref, wcat_ref, br_ref, tri_ref,
                     h_ref, xn_ref, slab_ref, slabt_ref, cnt_ref,
                     ext_ref, conv_ref, act_ref, carry_ref):
    b = pl.program_id(0)
    i = pl.program_id(1)
    tt = u_ref.shape[1]

    @pl.when((b == 0) & (i == 0))
    def _():
        carry_ref[...] = jnp.zeros_like(carry_ref)

    halo = halo_ref[0]
    ext_ref[0:CONV_HALO, :] = jnp.where(i == 0, jnp.zeros_like(halo), halo)
    ext_ref[CONV_HALO:CONV_HALO + tt, :] = u_ref[0]
    ext_ref[CONV_HALO + tt:, :] = jnp.zeros((CONV_SUBLANES, ext_ref.shape[1]), F32)
    lead = CONV_HALO - (CONV_WIDTH - 1)

    sub = CONV_SUBLANES
    for lt in range(ext_ref.shape[1] // LANES):
        cols = slice(lt * LANES, (lt + 1) * LANES)

        def conv_block(c, carry, cols=cols):
            base = pl.multiple_of(c * CONV_ROWS, CONV_ROWS)
            acc = jnp.zeros((CONV_ROWS, LANES), F32)
            for rem in range(sub):
                part = None
                for a in range(CONV_HALO // sub + 1):
                    k = rem + sub * a - lead
                    if 0 <= k < CONV_WIDTH:
                        term = wdw_ref[k:k + 1, cols] * ext_ref[pl.ds(base + sub * a, CONV_ROWS + sub), cols]
                        part = term if part is None else part + term
                acc = acc + part[rem:rem + CONV_ROWS]
            conv_ref[pl.ds(base, CONV_ROWS), cols] = acc
            return carry

        lax.fori_loop(0, tt // CONV_ROWS, conv_block, 0)

    def chunk(c, carry):
        base = pl.multiple_of(c * CONV_CHUNK, CONV_CHUNK)
        acc = conv_ref[pl.ds(base, CONV_CHUNK), :] + bdw_ref[...]
        mu = jnp.mean(acc, axis=-1, keepdims=True)
        cen = acc - mu
        var = jnp.mean(cen * cen, axis=-1, keepdims=True)
        y = cen * lax.rsqrt(var + EPS) * lng_ref[...] + lnb_ref[...]
        act_ref[pl.ds(base, CONV_CHUNK), :] = (y * jax.nn.sigmoid(y)).astype(BF16)
        return carry

    lax.fori_loop(0, tt // CONV_CHUNK, chunk, 0)

    h = x_ref[0] + _dot(act_ref[...], w2_ref[...]) + b2_ref[...]
    h_ref[0] = h
    xn, slab = _route(h, gffn_ref[...], wcat_ref, br_ref, tri_ref, carry_ref)
    xn_ref[...] = xn
    slab_ref[...] = slab
    slabt_ref[...] = slab.T[:ROUTE_FIELDS]
    cnt_ref[...] = carry_ref[...]


def _conv_mix(u, x, w_dw, b_dw, ln_g, ln_b, w_pw2, b_pw2, g_ffn, w_rg, b_rg, w_re, b_re):
    bsz, t, d = x.shape
    tt = ROW_TILE
    nt = t // tt
    n = bsz * t
    wcat, br, tri = _router_operands(w_rg, b_rg, w_re, b_re, tt)
    wdw = jnp.zeros((CONV_HALO, d), F32).at[:CONV_WIDTH].set(w_dw)
    hb = tt // CONV_HALO
    row = lambda v: v.reshape(1, d)
    return pl.pallas_call(
        _conv_mix_kernel,
        grid=(bsz, nt),
        in_specs=[
            pl.BlockSpec((1, tt, d), lambda b, i: (b, i, 0)),
            pl.BlockSpec((1, CONV_HALO, d), lambda b, i: (b, jnp.maximum(i * hb - 1, 0), 0)),
            pl.BlockSpec((1, tt, d), lambda b, i: (b, i, 0)),
            _full((CONV_HALO, d)), _full((1, d)), _full((1, d)), _full((1, d)),
            _full((d, d)), _full((1, d)), _full((1, d)),
            _full((d, 2 * LANES)), _full((1, LANES)), _full((tt, tt)),
        ],
        out_specs=[
            pl.BlockSpec((1, tt, d), lambda b, i: (b, i, 0)),
            pl.BlockSpec((tt, d), lambda b, i: (b * nt + i, 0)),
            pl.BlockSpec((tt, LANES), lambda b, i: (b * nt + i, 0)),
            pl.BlockSpec((ROUTE_FIELDS, tt), lambda b, i: (0, b * nt + i)),
            _full((1, LANES)),
        ],
        out_shape=[
            jax.ShapeDtypeStruct((bsz, t, d), F32),
            jax.ShapeDtypeStruct((n, d), F32),
            jax.ShapeDtypeStruct((n, LANES), F32),
            jax.ShapeDtypeStruct((ROUTE_FIELDS, n), F32),
            jax.ShapeDtypeStruct((1, LANES), F32),
        ],
        scratch_shapes=[pltpu.VMEM((CONV_HALO + tt + CONV_SUBLANES, d), F32), pltpu.VMEM((tt, d), F32),
                        pltpu.VMEM((tt, d), BF16),
                        pltpu.VMEM((1, LANES), F32)],
        compiler_params=_params(2),
        name="conv_mix",
    )(u, u, x, wdw, row(b_dw), row(ln_g), row(ln_b), w_pw2.astype(BF16), row(b_pw2), row(g_ffn), wcat, br, tri)


def _slot_assign_kernel(dest_ref, meta_ref, out_ref):
    n_asg = dest_ref.shape[0]
    groups = meta_ref.shape[0] // 3

    def pad_group(e, carry):
        off = meta_ref[2 * groups + e]

        def body(s, c):
            out_ref[s] = s + off
            return c

        lax.fori_loop(meta_ref[e], meta_ref[groups + e], body, 0)
        return carry

    lax.fori_loop(0, groups, pad_group, 0)

    def real(a, carry):
        out_ref[dest_ref[a]] = a
        return carry

    lax.fori_loop(0, n_asg, real, 0, unroll=8)


def _slot_assignments(dest_flat, meta, n_slots):
    smem = pl.BlockSpec(memory_space=pltpu.SMEM)
    return pl.pallas_call(
        _slot_assign_kernel,
        in_specs=[smem, smem],
        out_specs=smem,
        out_shape=jax.ShapeDtypeStruct((n_slots,), I32),
        name="moe_slot_assignments",
    )(dest_flat, meta)


def _expert_fused_kernel(be_ref, nu_ref, asg_ref, xn_hbm, wg_ref, wu_ref, wd_ref, y_hbm,
                         xbuf, ybuf, wgb_ref, wub_ref, wdb_ref, gsem, ssem, *, n_tok):
    i = pl.program_id(0)
    nu = nu_ref[0]
    be = xbuf.shape[1]
    slot = i % 2
    other = 1 - slot

    def gather_copy(blk, buf, r):
        a = asg_ref[blk * be + r]
        tok = a & (n_tok - 1) if n_tok & (n_tok - 1) == 0 else lax.rem(a, n_tok)
        return pltpu.make_async_copy(xn_hbm.at[pl.ds(tok, 1)], xbuf.at[buf, pl.ds(r, 1)], gsem.at[buf])

    def scatter_copy(blk, buf, r):
        return pltpu.make_async_copy(ybuf.at[buf, pl.ds(r, 1)], y_hbm.at[pl.ds(asg_ref[blk * be + r], 1)],
                                     ssem.at[buf])

    def for_rows(fn):
        def body(r, carry):
            fn(r)
            return carry

        lax.fori_loop(0, be, body, 0, unroll=8)

    @pl.when(i == 0)
    def _():
        for_rows(lambda r: gather_copy(0, 0, r).start())
        ybuf[1] = jnp.zeros(ybuf.shape[1:], ybuf.dtype)

    @pl.when(i < nu)
    def _():
        for_rows(lambda r: gather_copy(0, slot, r).wait())

        @pl.when((i == 0) | (be_ref[i] != be_ref[jnp.maximum(i - 1, 0)]))
        def _():
            wgb_ref[...] = wg_ref[0, 0].astype(BF16)
            wub_ref[...] = wu_ref[0, 0].astype(BF16)
            wdb_ref[...] = wd_ref[0, 0].astype(BF16)

        nxt = jnp.minimum(i + 1, nu - 1)
        prv = jnp.maximum(i - 1, 0)
        for r in range(be):
            gather_copy(nxt, other, r).start()
            scatter_copy(prv, other, r).start()
        x = xbuf[slot].astype(BF16)
        a = _dot(x, wgb_ref[...])
        u = _dot(x, wub_ref[...])
        hid = (a * jax.nn.sigmoid(a) * u).astype(BF16)
        ybuf[slot] = _dot(hid, wdb_ref[...])
        for_rows(lambda r: scatter_copy(0, other, r).wait())

    @pl.when(i == nu)
    def _():
        for_rows(lambda r: gather_copy(0, slot, r).wait())
        for_rows(lambda r: scatter_copy(nu - 1, other, r).start())
        for_rows(lambda r: scatter_copy(0, other, r).wait())
        xbuf[slot] = jnp.zeros(xbuf.shape[1:], xbuf.dtype)

        def zero_copy(blk):
            return pltpu.make_async_copy(xbuf.at[slot], y_hbm.at[pl.ds(pl.multiple_of(blk * be, be), be)],
                                         gsem.at[slot])

        def start(blk, carry):
            zero_copy(blk).start()
            return carry

        def finish(blk, carry):
            zero_copy(blk).wait()
            return carry

        lax.fori_loop(nu, pl.num_programs(0), start, 0)
        lax.fori_loop(nu, pl.num_programs(0), finish, 0)


def _experts_fused(blk_e, n_used, slot_asg, xn, layer, w_gate, w_up, w_down):
    n, d = xn.shape
    f = w_gate.shape[3]
    be = EXPERT_ROWS
    n_slots = slot_asg.shape[0]
    wspec = lambda a, b: pl.BlockSpec((1, 1, a, b), lambda i, e, nu, asg: (layer, e[i], 0, 0))
    return pl.pallas_call(
        functools.partial(_expert_fused_kernel, n_tok=n),
        grid_spec=pltpu.PrefetchScalarGridSpec(
            num_scalar_prefetch=3,
            grid=(n_slots // be,),
            in_specs=[pl.BlockSpec(memory_space=pl.ANY), wspec(d, f), wspec(d, f), wspec(f, d)],
            out_specs=pl.BlockSpec(memory_space=pl.ANY),
            scratch_shapes=[pltpu.VMEM((2, be, d), F32), pltpu.VMEM((2, be, d), F32),
                            pltpu.VMEM((d, f), BF16), pltpu.VMEM((d, f), BF16), pltpu.VMEM((f, d), BF16),
                            pltpu.SemaphoreType.DMA((2,)), pltpu.SemaphoreType.DMA((2,))],
        ),
        out_shape=jax.ShapeDtypeStruct((n_slots, d), F32),
        compiler_params=_params(1),
        name="moe_experts",
    )(blk_e, n_used, slot_asg, xn, w_gate, w_up, w_down)


def _ple_kernel(h_ref, y0_ref, y1_ref, slab_ref, p_ref, gple_ref, wg_ref, wp_ref, o_ref):
    slab = slab_ref[...]
    lane = lax.broadcasted_iota(I32, slab.shape, 1)
    w0 = jnp.sum(jnp.where(lane == 2, slab, 0.0), axis=-1, keepdims=True)
    w1 = jnp.sum(jnp.where(lane == 3, slab, 0.0), axis=-1, keepdims=True)
    hm = h_ref[...] + w0 * y0_ref[...] + w1 * y1_ref[...]
    gate = jax.nn.sigmoid(_dot(_rms(hm, gple_ref[...]).astype(BF16), wg_ref[...]))
    o_ref[...] = hm + gate * _dot(p_ref[0].astype(BF16), wp_ref[...])


def _ple(h2, y, slab, layer, p3, g_ple, w_gate, w_proj):
    n, d = h2.shape
    dp = p3.shape[2]
    tm = ROW_TILE
    nt = n // tm
    row = lambda w: pl.BlockSpec((tm, w), lambda i: (i, 0))
    return pl.pallas_call(
        _ple_kernel,
        grid=(nt,),
        in_specs=[row(d), row(d), pl.BlockSpec((tm, d), lambda i: (nt + i, 0)), row(LANES),
                  pl.BlockSpec((1, tm, dp), lambda i: (layer, i, 0)), _full((1, d)), _full((d, d)), _full((dp, d))],
        out_specs=row(d),
        out_shape=jax.ShapeDtypeStruct((n, d), F32),
        compiler_params=_params(1),
        name="moe_combine_ple",
    )(h2, y, y, slab, p3, g_ple.reshape(1, d), w_gate.astype(BF16), w_proj.astype(BF16))


def _moe_ple(h2, xn, slab, slab_t, counts, layer, p3, g_ple, w_gate_e, w_up_e, w_down_e, ple_w_gate, ple_w_proj):
    n, d = h2.shape
    be = EXPERT_ROWS
    n_blocks = (2 * n) // be + N_EXPERTS
    cnt = counts[0, :N_EXPERTS].astype(I32)
    padded = (cnt + be - 1) // be * be
    ends = jnp.cumsum(padded)
    starts = ends - padded
    eid = slab_t[0:2].astype(I32)
    rank = slab_t[4:6].astype(I32)
    expert_ids = jnp.arange(N_EXPERTS, dtype=I32).reshape(N_EXPERTS, 1, 1)
    slot_base = jnp.sum(jnp.where(eid[None] == expert_ids, starts.reshape(N_EXPERTS, 1, 1), 0), axis=0)
    dest_flat = (slot_base + rank).reshape(2 * n)
    block_start = jnp.arange(n_blocks, dtype=I32) * be
    blk_e = jnp.minimum(jnp.sum((ends[None, :] <= block_start[:, None]).astype(I32), axis=1), N_EXPERTS - 1)
    n_used = (ends[-1:] // be).astype(I32)
    seen = jnp.cumsum(cnt)
    pad_lo = jnp.concatenate([starts + cnt, ends[-1:]])
    pad_hi = jnp.concatenate([ends, jnp.full((1,), n_blocks * be, I32)])
    pad_off = jnp.concatenate([2 * n - seen, jnp.zeros((1,), I32)])
    meta = jnp.concatenate([pad_lo, pad_hi, pad_off]).astype(I32)
    slot_asg = _slot_assignments(dest_flat, meta, n_blocks * be)
    y = _experts_fused(blk_e, n_used, slot_asg, xn, layer, w_gate_e, w_up_e, w_down_e)
    return _ple(h2, y, slab, layer, p3, g_ple, ple_w_gate, ple_w_proj)


def _rope_table_kernel(pos_ref, inv_ref, sign_ref, cos_ref, sin_ref):
    ang = pos_ref[...] * inv_ref[...]
    cos_ref[...] = jnp.cos(ang)
    sin_ref[...] = jnp.sin(ang) * sign_ref[...]


def _rope_table(positions):
    n = positions.size
    tm = ROW_TILE
    half = HEAD_DIM // 2
    lane = jnp.arange(LANES)
    inv = 1.0 / (ROPE_THETA ** ((lane % half).astype(F32) / half))
    sign = jnp.where(lane % HEAD_DIM < half, -1.0, 1.0).astype(F32)
    return pl.pallas_call(
        _rope_table_kernel,
        grid=(n // tm,),
        in_specs=[pl.BlockSpec((tm, 1), lambda i: (i, 0)), _full((1, LANES)), _full((1, LANES))],
        out_specs=[pl.BlockSpec((tm, LANES), lambda i: (i, 0))] * 2,
        out_shape=[jax.ShapeDtypeStruct((n, LANES), F32)] * 2,
        compiler_params=_params(1),
        name="rope_table",
    )(positions.astype(F32).reshape(n, 1), inv.reshape(1, LANES), sign.reshape(1, LANES))


def _head_norm_rope(z, gain, seg_ref, cos, sin):
    z2 = z * z
    hi = z2.astype(BF16)
    lo = (z2 - hi.astype(F32)).astype(BF16)
    ssq = _dot(hi, seg_ref[...]) + _dot(lo, seg_ref[...])
    zn = z * lax.rsqrt(ssq * (1.0 / HEAD_DIM) + EPS) * gain
    width = z.shape[1]
    half = HEAD_DIM // 2
    lane = lax.broadcasted_iota(I32, z.shape, 1)
    rot = jnp.where((lane & half) == 0, pltpu.roll(zn, width - half, 1), pltpu.roll(zn, half, 1))
    return zn * cos + rot * sin


def _nsa_proj_kernel(h_ref, g_ref, w_ref, cos_ref, sin_ref, qn_ref, kns_ref, knw_ref, seg_ref,
                     q_ref, kc_ref, vc_ref, ks_ref, vs_ref, kw_ref, vw_ref, gate_ref):
    d = h_ref.shape[1]
    kd = N_KV * HEAD_DIM
    xn = _rms(h_ref[...], g_ref[...]).astype(BF16)
    cos = jnp.concatenate([cos_ref[...], cos_ref[...]], axis=1)
    sin = jnp.concatenate([sin_ref[...], sin_ref[...]], axis=1)
    scale = HEAD_DIM ** -0.5 * LOG2_E

    def heads_out(ref, first, z):
        ones = jnp.ones((z.shape[0], HEAD_DIM), ref.dtype)
        for r in range(N_KV):
            zr = z[:, r * HEAD_DIM:(r + 1) * HEAD_DIM].astype(ref.dtype)
            ref[0, first + r] = zr if ref.shape[-1] == HEAD_DIM else jnp.concatenate([zr, ones], axis=1)

    for j in range(d // kd):
        z = _dot(xn, w_ref[:, j * kd:(j + 1) * kd])
        heads_out(q_ref, N_KV * j, _head_norm_rope(z, qn_ref[...], seg_ref, cos, sin) * scale)
    col = lambda c: _dot(xn, w_ref[:, d + c * kd:d + (c + 1) * kd])
    heads_out(kc_ref, 0, col(0))
    heads_out(vc_ref, 0, col(1))
    heads_out(ks_ref, 0, _head_norm_rope(col(2), kns_ref[...], seg_ref, cos, sin))
    heads_out(vs_ref, 0, col(3))
    heads_out(kw_ref, 0, _head_norm_rope(col(4), knw_ref[...], seg_ref, cos, sin))
    heads_out(vw_ref, 0, col(5))
    gate_ref[...] = jax.nn.sigmoid(_dot(xn, w_ref[:, d + 6 * kd:]))


def _nsa_proj(h, g_mix, w_in, cos_t, sin_t, q_norm, k_norm):
    bsz, t, d = h.shape
    n = bsz * t
    tm = ROW_TILE
    nt = t // tm
    kd = N_KV * HEAD_DIM
    n_heads = d // HEAD_DIM
    n_in = w_in.shape[1]
    w_pad = jnp.zeros((d, d + 6 * kd + LANES), F32).at[:, :n_in].set(w_in).astype(BF16)
    tile4 = lambda v: jnp.tile(v, kd // HEAD_DIM).reshape(1, kd)
    idx = jnp.arange(kd) // HEAD_DIM
    seg = (idx[:, None] == idx[None, :]).astype(BF16)
    head_spec = lambda nh, w=HEAD_DIM: pl.BlockSpec((1, nh, tm, w), lambda i: (i // nt, 0, i % nt, 0))
    head_shape = lambda nh, w=HEAD_DIM, dt=BF16: jax.ShapeDtypeStruct((bsz, nh, t, w), dt)
    flat_spec = lambda w: pl.BlockSpec((tm, w), lambda i: (i, 0))
    vw2 = 2 * HEAD_DIM
    return pl.pallas_call(
        _nsa_proj_kernel,
        grid=(n // tm,),
        in_specs=[flat_spec(d), _full((1, d)), _full(w_pad.shape), flat_spec(LANES), flat_spec(LANES),
                  _full((1, kd)), _full((1, kd)), _full((1, kd)), _full((kd, kd))],
        out_specs=[head_spec(n_heads), head_spec(N_KV), head_spec(N_KV), head_spec(N_KV), head_spec(N_KV, vw2),
                   head_spec(N_KV), head_spec(N_KV, vw2), flat_spec(LANES)],
        out_shape=[head_shape(n_heads), head_shape(N_KV, dt=F32), head_shape(N_KV, dt=F32),
                   head_shape(N_KV), head_shape(N_KV, vw2), head_shape(N_KV), head_shape(N_KV, vw2),
                   jax.ShapeDtypeStruct((n, LANES), F32)],
        compiler_params=_params(1),
        name="nsa_proj",
    )(h.reshape(n, d), g_mix.reshape(1, d), w_pad, cos_t, sin_t,
      tile4(q_norm), tile4(k_norm[1]), tile4(k_norm[2]), seg)


def _compress_kernel(xk_ref, xv_ref, pe_ref, w1_ref, w2_ref, kn_ref, cos_ref, sin_ref, o_ref):
    nch, dh = o_ref.shape[3], o_ref.shape[4]
    st = CMP_STRIDE
    hidden = w1_ref.shape[2]
    for which, x_ref in enumerate((xk_ref, xv_ref)):
        first = jnp.zeros((nch, hidden), F32)
        second = jnp.zeros((nch, hidden), F32)
        for tp in range(st):
            xt = x_ref[0, 0, pl.ds(tp, nch, stride=st), :]
            first = first + _dot((xt + pe_ref[which, tp:tp + 1, :]).astype(BF16),
                                 w1_ref[which, tp * dh:(tp + 1) * dh, :])
            second = second + _dot((xt + pe_ref[which, st + tp:st + tp + 1, :]).astype(BF16),
                                   w1_ref[which, (st + tp) * dh:(st + tp + 1) * dh, :])
        hid = first + pltpu.roll(second, nch - 1, 0)
        c = _dot(jax.nn.gelu(hid, approximate=True).astype(BF16), w2_ref[which])
        if which == 0:
            half = dh // 2
            cn = _rms(c, kn_ref[...])
            rot = jnp.concatenate([cn[:, half:], cn[:, :half]], axis=1)
            c = cn * cos_ref[0] + rot * sin_ref[0]
        o_ref[which, 0, 0] = c.astype(o_ref.dtype)


def _compress(kc_raw, vc_raw, pe, w1, w2, k_norm0, cos_t, sin_t):
    bsz, g, t, dh = kc_raw.shape
    st = CMP_STRIDE
    nch = t // st
    last = lambda tab: jnp.pad(tab.reshape(bsz, t, LANES)[:, CMP_BLOCK - 1::st, :dh], ((0, 0), (0, 1), (0, 0)))
    src = pl.BlockSpec((1, 1, t, dh), lambda b, gi: (b, gi, 0, 0))
    tab = pl.BlockSpec((1, nch, dh), lambda b, gi: (b, 0, 0))
    return pl.pallas_call(
        _compress_kernel,
        grid=(bsz, g),
        in_specs=[src, src, _full(pe.shape), _full(w1.shape), _full(w2.shape), _full((1, dh)), tab, tab],
        out_specs=pl.BlockSpec((2, 1, 1, nch, dh), lambda b, gi: (0, b, gi, 0, 0)),
        out_shape=jax.ShapeDtypeStruct((2, bsz, g, nch, dh), BF16),
        compiler_params=_params(2),
        name="nsa_compress",
    )(kc_raw, vc_raw, pe, w1.astype(BF16), w2.astype(BF16), k_norm0.reshape(1, dh), last(cos_t), last(sin_t))


def _attend_chunk(q, k, v, bias, m_ref, acc_ref, r_heads, tq_n):
    tk = k.shape[0]
    s = _dot_nt(q, k)
    m_all = m_ref[...]
    acc_all = acc_ref[...]
    m_out, scaled, probs = [], [], []
    for r in range(r_heads):
        rs = slice(r * tq_n, (r + 1) * tq_n)
        sr = s[rs] + bias
        m_old = m_all[rs]
        m_new = jnp.maximum(m_old, jnp.max(sr, axis=-1, keepdims=True))
        m_wide = m_new if tk == m_new.shape[1] else jnp.concatenate([m_new] * (tk // m_new.shape[1]), axis=1)
        probs.append(jnp.exp2(sr - m_wide).astype(BF16))
        m_out.append(m_new)
        scaled.append(jnp.exp2(m_old - m_new) * acc_all[rs])
    m_ref[...] = jnp.concatenate(m_out, axis=0)
    acc_ref[...] = jnp.concatenate(scaled, axis=0) + _dot(jnp.concatenate(probs, axis=0), v)


def _nsa_attn_kernel(q_ref, kc_ref, vc_ref, ks_ref, vs_ref, kw_ref, vw_ref, gate_ref, wsel_ref, e_ref, o_ref,
                     m_ref, acc_ref, og_ref, bias_ref, *, n_sel):
    g = pl.program_id(1)
    qi = pl.program_id(2)
    r_heads, tq_n, dh = q_ref.shape[1], q_ref.shape[2], q_ref.shape[3]
    rows = r_heads * tq_n
    t0 = qi * tq_n
    tk = K_TILE
    q = q_ref[0].reshape(rows, dh)
    head_rows = [slice(r * tq_n, (r + 1) * tq_n) for r in range(r_heads)]

    gates = gate_ref[0]
    gate_lane = lax.broadcasted_iota(I32, gates.shape, 1)

    def gate(r, branch):
        col = (g * r_heads + r) * 3 + branch
        return jnp.sum(jnp.where(gate_lane == col, gates, 0.0), axis=-1, keepdims=True)

    def restart():
        m_ref[...] = jnp.full(m_ref.shape, NEG, F32)
        acc_ref[...] = jnp.zeros(acc_ref.shape, F32)

    def add_branch(branch):
        acc = acc_ref[...]
        o = acc * (1.0 / pltpu.roll(acc, dh, 1))
        og_ref[...] += jnp.concatenate([gate(r, branch) * o[rs, :dh] for r, rs in enumerate(head_rows)], axis=0)

    kc = kc_ref[0, 0]
    nc = kc.shape[0]
    tq_c = t0 + lax.broadcasted_iota(I32, (tq_n, nc), 0)
    mask_c = CMP_STRIDE * lax.broadcasted_iota(I32, (tq_n, nc), 1) + (CMP_BLOCK - 1) <= tq_c
    s_c = _dot_nt(q, kc)
    imp = jnp.zeros((tq_n, nc), F32)
    probs = []
    for rs in head_rows:
        sr = jnp.where(mask_c, s_c[rs], NEG)
        e = jnp.where(mask_c, jnp.exp2(sr - jnp.max(sr, axis=-1, keepdims=True)), 0.0)
        l = jnp.sum(e, axis=-1, keepdims=True)
        p = e * jnp.where(l > 0.0, 1.0 / l, 0.0)
        imp = imp + p
        probs.append(p.astype(BF16))
    o_c = _dot(jnp.concatenate(probs, axis=0), vc_ref[0, 0])
    og_ref[...] = jnp.concatenate([gate(r, 0) * o_c[rs] for r, rs in enumerate(head_rows)], axis=0)

    imp_hi = imp.astype(BF16)
    imp_lo = (imp - imp_hi.astype(F32)).astype(BF16)
    p_slc = _dot_nt(wsel_ref[...], imp_hi) + _dot_nt(wsel_ref[...], imp_lo)
    nsb = p_slc.shape[0]
    blk = lax.broadcasted_iota(I32, (nsb, tq_n), 0)
    tq_t = t0 + lax.broadcasted_iota(I32, (nsb, tq_n), 1)
    cur = tq_t // SLC_BLOCK
    forced = (blk == 0) | (blk == cur) | (blk == cur - 1)
    score = jnp.where(forced, jnp.inf, jnp.where(blk * SLC_BLOCK <= tq_t, p_slc, -jnp.inf))
    cnt = jnp.zeros((nsb, tq_n), F32)
    for i in range(nsb):
        ri = score[i:i + 1, :]
        beats = (ri > score) | ((ri == score) & (blk > i))
        cnt = cnt + jnp.where(beats, 1.0, 0.0)
    dropped = jnp.where(cnt < n_sel, 0.0, 1.0)
    dropped = jnp.concatenate([dropped, jnp.zeros((LANES - nsb, tq_n), F32)], axis=0).T.astype(BF16)

    kwn = WINDOW + tq_n
    start = pl.multiple_of(jnp.maximum(t0 - WINDOW, 0), tq_n)
    kpos_w = start + lax.broadcasted_iota(I32, (tq_n, kwn), 1)
    tq_w = t0 + lax.broadcasted_iota(I32, (tq_n, kwn), 0)
    bias_w = jnp.where((kpos_w <= tq_w) & (kpos_w > tq_w - WINDOW), 0.0, NEG)
    s_w = _dot_nt(q, kw_ref[0, 0, pl.ds(start, kwn), :])
    probs = []
    for rs in head_rows:
        sr = s_w[rs] + bias_w
        probs.append(jnp.exp2(sr - jnp.max(sr, axis=-1, keepdims=True)).astype(BF16))
    acc_w = _dot(jnp.concatenate(probs, axis=0), vw_ref[0, 0, pl.ds(start, kwn), :])
    o_w = acc_w * (1.0 / pltpu.roll(acc_w, dh, 1))
    og_ref[...] += jnp.concatenate([gate(r, 2) * o_w[rs, :dh] for r, rs in enumerate(head_rows)], axis=0)

    t_keys = e_ref.shape[1]
    kpos = lax.broadcasted_iota(I32, (tq_n, t_keys), 1)
    tq_s = t0 + lax.broadcasted_iota(I32, (tq_n, t_keys), 0)
    bias_all = jnp.where(kpos <= tq_s, _dot(dropped, e_ref[...]), NEG)
    for c in range(t_keys // tk):
        bias_ref[c] = bias_all[:, c * tk:(c + 1) * tk]
    restart()

    def sel_step(c, carry):
        k0 = pl.multiple_of(c * tk, tk)
        _attend_chunk(q, ks_ref[0, 0, pl.ds(k0, tk), :], vs_ref[0, 0, pl.ds(k0, tk), :], bias_ref[c],
                      m_ref, acc_ref, r_heads, tq_n)
        return carry

    lax.fori_loop(0, (t0 + tq_n + tk - 1) // tk, sel_step, 0)
    add_branch(1)

    for r, rs in enumerate(head_rows):
        o_ref[0, :, r * dh:(r + 1) * dh] = og_ref[rs].astype(o_ref.dtype)


def _nsa_attn(q, kc, vc, ks, vs, kw, vw, gate, n_sel):
    bsz, n_heads, t, dh = q.shape
    g = N_KV
    r_heads = n_heads // g
    tq = Q_TILE
    nch = kc.shape[2]
    a = SLC_BLOCK // CMP_STRIDE
    bb = CMP_BLOCK // CMP_STRIDE
    nsb = t // SLC_BLOCK
    j = jnp.arange(nsb)[:, None]
    c = jnp.arange(nch)[None, :]
    wsel = sum(((c == a * j + m + n_) & (c < nch - 1)).astype(F32)
               for m in range(a) for n_ in range(bb)).astype(BF16)
    tk = K_TILE
    key_blk = (jnp.arange(t) // SLC_BLOCK).reshape(1, t)
    e = jnp.where(key_blk == jnp.arange(LANES).reshape(LANES, 1), NEG, 0.0).astype(BF16)
    rows = r_heads * tq
    kv_spec = lambda n_rows, w=dh: pl.BlockSpec((1, 1, n_rows, w), lambda b, gi, qi: (b, gi, 0, 0))
    return pl.pallas_call(
        functools.partial(_nsa_attn_kernel, n_sel=n_sel),
        grid=(bsz, g, t // tq),
        in_specs=[
            pl.BlockSpec((1, r_heads, tq, dh), lambda b, gi, qi: (b, gi, qi, 0)),
            kv_spec(nch), kv_spec(nch), kv_spec(t), kv_spec(t, 2 * dh), kv_spec(t), kv_spec(t, 2 * dh),
            pl.BlockSpec((1, tq, LANES), lambda b, gi, qi: (b, qi, 0)),
            pl.BlockSpec((nsb, nch), lambda b, gi, qi: (0, 0)),
            pl.BlockSpec(e.shape, lambda b, gi, qi: (0, 0)),
        ],
        out_specs=pl.BlockSpec((1, tq, r_heads * dh), lambda b, gi, qi: (b, qi, gi)),
        out_shape=jax.ShapeDtypeStruct((bsz, t, n_heads * dh), BF16),
        scratch_shapes=[pltpu.VMEM((rows, 2 * dh), F32), pltpu.VMEM((rows, 2 * dh), F32),
                        pltpu.VMEM((rows, dh), F32), pltpu.VMEM((t // tk, tq, tk), F32)],
        compiler_params=_params(3),
        name="nsa_attention",
    )(q, kc, vc, ks, vs, kw, vw, gate, wsel, e)


def _out_mix_kernel(o_ref, w_ref, h_ref, gffn_ref, wcat_ref, br_ref, tri_ref,
                    h2_ref, xn_ref, slab_ref, slabt_ref, cnt_ref, carry_ref):
    @pl.when(pl.program_id(0) == 0)
    def _():
        carry_ref[...] = jnp.zeros_like(carry_ref)

    h = h_ref[...] + _dot(o_ref[...], w_ref[...])
    h2_ref[...] = h
    xn, slab = _route(h, gffn_ref[...], wcat_ref, br_ref, tri_ref, carry_ref)
    xn_ref[...] = xn
    slab_ref[...] = slab
    slabt_ref[...] = slab.T[:ROUTE_FIELDS]
    cnt_ref[...] = carry_ref[...]


def _out_mix(o2, w_out, h2, g_ffn, w_rg, b_rg, w_re, b_re):
    n, d = h2.shape
    tm = ROW_TILE
    wcat, br, tri = _router_operands(w_rg, b_rg, w_re, b_re, tm)
    row = pl.BlockSpec((tm, d), lambda i: (i, 0))
    return pl.pallas_call(
        _out_mix_kernel,
        grid=(n // tm,),
        in_specs=[row, _full((d, d)), row, _full((1, d)), _full((d, 2 * LANES)), _full((1, LANES)), _full((tm, tm))],
        out_specs=[row, row, pl.BlockSpec((tm, LANES), lambda i: (i, 0)),
                   pl.BlockSpec((ROUTE_FIELDS, tm), lambda i: (0, i)), _full((1, LANES))],
        out_shape=[jax.ShapeDtypeStruct((n, d), F32), jax.ShapeDtypeStruct((n, d), F32),
                   jax.ShapeDtypeStruct((n, LANES), F32), jax.ShapeDtypeStruct((ROUTE_FIELDS, n), F32),
                   jax.ShapeDtypeStruct((1, LANES), F32)],
        scratch_shapes=[pltpu.VMEM((1, LANES), F32)],
        compiler_params=_params(1),
        name="nsa_out_mix",
    )(o2, w_out.astype(BF16), h2, g_ffn.reshape(1, d), wcat, br, tri)


def _nsa_layer(h, positions, g_mix, w_in, q_norm, k_norm, cmp_pe, cmp_w1, cmp_w2):
    bsz, t, d = h.shape
    cos_t, sin_t = _rope_table(positions)
    q, kc_raw, vc_raw, ks, vs, kw, vw, gate = _nsa_proj(h, g_mix, w_in, cos_t, sin_t, q_norm, k_norm)
    cmp = _compress(kc_raw, vc_raw, cmp_pe, cmp_w1, cmp_w2, k_norm[0], cos_t, sin_t)
    n_sel = min(N_SEL, t // SLC_BLOCK)
    return _nsa_attn(q, cmp[0], cmp[1], ks, vs, kw, vw, gate.reshape(bsz, t, LANES), n_sel)


def kernel(x, p, positions, g_mix, g_ffn, g_ple, conv_w_pw1, conv_b_pw1, conv_w_dw, conv_b_dw, conv_ln_g, conv_ln_b, conv_w_pw2, conv_b_pw2, nsa_w_in, nsa_q_norm, nsa_k_norm, nsa_cmp_pe, nsa_cmp_w1, nsa_cmp_w2, nsa_w_out, moe_w_rg, moe_b_rg, moe_w_re, moe_b_re, moe_w_gate, moe_w_up, moe_w_down, ple_w_proj, ple_w_gate):
    bsz, t, d = x.shape
    n = bsz * t
    depth = p.shape[0]
    h = x
    p3 = p.reshape(depth, n, p.shape[-1])
    for i in range(depth):
        j = i // 2
        route_w = (g_ffn[i], moe_w_rg[i], moe_b_rg[i], moe_w_re[i], moe_b_re[i])
        if i % 2 == 0:
            u = _pw1_glu(h.reshape(n, d), g_mix[i], conv_w_pw1[j], conv_b_pw1[j])
            h, xn, slab, slab_t, counts = _conv_mix(u.reshape(bsz, t, d), h, conv_w_dw[j], conv_b_dw[j],
                                                    conv_ln_g[j], conv_ln_b[j], conv_w_pw2[j], conv_b_pw2[j], *route_w)
        else:
            o = _nsa_layer(h, positions, g_mix[i], nsa_w_in[j], nsa_q_norm[j], nsa_k_norm[j],
                           nsa_cmp_pe[j], nsa_cmp_w1[j], nsa_cmp_w2[j])
            h, xn, slab, slab_t, counts = _out_mix(o.reshape(n, d), nsa_w_out[j], h.reshape(n, d), *route_w)
        h = _moe_ple(h.reshape(n, d), xn, slab, slab_t, counts, i, p3, g_ple[i],
                     moe_w_gate, moe_w_up, moe_w_down, ple_w_gate[i], ple_w_proj[i]).reshape(bsz, t, d)
    return h
```

```python
import functools

import jax
import jax.numpy as jnp
from jax import lax
from jax.experimental import pallas as pl
from jax.experimental.pallas import tpu as pltpu

F32 = jnp.float32
BF16 = jnp.bfloat16
I32 = jnp.int32

LANES = 128
VMEM_LIMIT = 56 * 1024 * 1024

EPS = 1e-6
NEG = -1e30
LOG2_E = 1.4426950408889634

CONV_WIDTH = 31
HEAD_DIM = 64
N_KV = 4
CMP_BLOCK = 32
CMP_STRIDE = 16
SLC_BLOCK = 64
N_SEL = 16
WINDOW = 512
ROPE_THETA = 10000.0
N_GROUPS = 4
EXPERTS_PER_GROUP = 8
N_EXPERTS = N_GROUPS * EXPERTS_PER_GROUP

ROW_TILE = 512
CONV_CHUNK = 64
CONV_ROWS = 128
CONV_SUBLANES = 8
CONV_HALO = 32
EXPERT_ROWS = 256
Q_TILE = 256
K_TILE = 512
ROUTE_FIELDS = 8


def _params(n_grid):
    return pltpu.CompilerParams(dimension_semantics=("arbitrary",) * n_grid,
                                vmem_limit_bytes=VMEM_LIMIT)


def _dot(a, b):
    return jnp.dot(a, b, preferred_element_type=F32)


def _dot_nt(a, b):
    return lax.dot_general(a, b, (((1,), (1,)), ((), ())), preferred_element_type=F32)


def _rms(x, g):
    return x * lax.rsqrt(jnp.mean(x * x, axis=-1, keepdims=True) + EPS) * g


def _full(shape):
    n = len(shape)
    return pl.BlockSpec(shape, lambda *_: (0,) * n)


def _route(h, g_ffn, wcat_ref, br_ref, tri_ref, carry_ref):
    xn = _rms(h, g_ffn)
    hi = xn.astype(BF16)
    lo = (xn - hi.astype(F32)).astype(BF16)
    r1 = _dot(hi, wcat_ref[...])
    r2 = _dot(lo, wcat_ref[:, :LANES])
    logits = r1[:, :LANES] + r1[:, LANES:] + r2 + br_ref[...]

    lane = lax.broadcasted_iota(I32, logits.shape, 1).astype(F32)
    lg = jnp.where(lane < N_GROUPS, logits, -jnp.inf)
    m = jnp.max(lg, axis=-1, keepdims=True)
    grp = jnp.min(jnp.where(lg == m, lane, float(LANES)), axis=-1, keepdims=True)
    pg_sel = 1.0 / jnp.sum(jnp.exp(lg - m), axis=-1, keepdims=True)

    lo_e = N_GROUPS + EXPERTS_PER_GROUP * grp
    le = jnp.where((lane >= lo_e) & (lane < lo_e + EXPERTS_PER_GROUP), logits, -jnp.inf)
    v1 = jnp.max(le, axis=-1, keepdims=True)
    i1 = jnp.min(jnp.where(le == v1, lane, float(LANES)), axis=-1, keepdims=True)
    le2 = jnp.where(lane == i1, -jnp.inf, le)
    v2 = jnp.max(le2, axis=-1, keepdims=True)
    i2 = jnp.min(jnp.where(le2 == v2, lane, float(LANES)), axis=-1, keepdims=True)
    e21 = jnp.exp(v2 - v1)
    w1 = pg_sel / (1.0 + e21)
    w2 = pg_sel * e21 / (1.0 + e21)
    e1 = i1 - N_GROUPS
    e2 = i2 - N_GROUPS

    oh1 = jnp.where(lane == e1, 1.0, 0.0)
    oh2 = jnp.where(lane == e2, 1.0, 0.0)
    c1 = _dot(tri_ref[...], oh1.astype(BF16))
    c2 = _dot(tri_ref[...], oh2.astype(BF16))
    carry = carry_ref[...]
    tot1 = jnp.sum(oh1, axis=0, keepdims=True)
    tot2 = jnp.sum(oh2, axis=0, keepdims=True)
    rank1 = jnp.sum(oh1 * (c1 + carry), axis=-1, keepdims=True)
    rank2 = jnp.sum(oh2 * (c2 + carry + tot1), axis=-1, keepdims=True)
    carry_ref[...] = carry + tot1 + tot2

    slab = jnp.where(lane == 0, e1, jnp.where(lane == 1, e2, jnp.where(lane == 2, w1, jnp.where(
        lane == 3, w2, jnp.where(lane == 4, rank1, jnp.where(lane == 5, rank2, 0.0))))))
    return xn, slab


def _router_operands(w_rg, b_rg, w_re, b_re, tm):
    d = w_rg.shape[0]
    w = jnp.zeros((d, LANES), F32).at[:, :N_GROUPS].set(w_rg).at[:, N_GROUPS:N_GROUPS + N_EXPERTS].set(w_re)
    w_hi = w.astype(BF16)
    w_lo = (w - w_hi.astype(F32)).astype(BF16)
    wcat = jnp.concatenate([w_hi, w_lo], axis=1)
    br = jnp.zeros((1, LANES), F32).at[0, :N_GROUPS].set(b_rg).at[0, N_GROUPS:N_GROUPS + N_EXPERTS].set(b_re)
    r = lax.broadcasted_iota(I32, (tm, tm), 0)
    c = lax.broadcasted_iota(I32, (tm, tm), 1)
    tri = (c < r).astype(BF16)
    return wcat, br, tri


def _pw1_glu_kernel(x_ref, g_ref, w_ref, b_ref, o_ref, *, chunk):
    xn = _rms(x_ref[...], g_ref[...]).astype(BF16)
    d = o_ref.shape[1]
    for j in range(d // chunk):
        sa = slice(j * chunk, (j + 1) * chunk)
        sg = slice(d + j * chunk, d + (j + 1) * chunk)
        a = _dot(xn, w_ref[:, sa]) + b_ref[:, sa]
        g = _dot(xn, w_ref[:, sg]) + b_ref[:, sg]
        o_ref[:, sa] = a * jax.nn.sigmoid(g)


def _pw1_glu(x2, g, w, b):
    n, d = x2.shape
    tm = ROW_TILE
    return pl.pallas_call(
        functools.partial(_pw1_glu_kernel, chunk=512),
        grid=(n // tm,),
        in_specs=[pl.BlockSpec((tm, d), lambda i: (i, 0)), _full((1, d)), _full((d, 2 * d)), _full((1, 2 * d))],
        out_specs=pl.BlockSpec((tm, d), lambda i: (i, 0)),
        out_shape=jax.ShapeDtypeStruct((n, d), F32),
        compiler_params=_params(1),
        name="pw1_glu",
    )(x2, g.reshape(1, d), w.astype(BF16), b.reshape(1, 2 * d))


def _conv_mix_kernel(u_ref, halo_ref, x_ref, wdw_ref, bdw_ref, lng_ref, lnb_ref, w2_ref, b2_ref,
                     gffn_ref, wcat_ref, br_ref, tri_ref,
                     h_ref, xn_ref, slab_ref, slabt_ref, cnt_ref,
                     ext_ref, conv_ref, act_ref, carry_ref):
    b = pl.program_id(0)
    i = pl.program_id(1)
    tt = u_ref.shape[1]

    @pl.when((b == 0) & (i == 0))
    def _():
        carry_ref[...] = jnp.zeros_like(carry_ref)

    halo = halo_ref[0]
    ext_ref[0:CONV_HALO, :] = jnp.where(i == 0, jnp.zeros_like(halo), halo)
    ext_ref[CONV_HALO:CONV_HALO + tt, :] = u_ref[0]
    ext_ref[CONV_HALO + tt:, :] = jnp.zeros((CONV_SUBLANES, ext_ref.shape[1]), F32)
    lead = CONV_HALO - (CONV_WIDTH - 1)

    sub = CONV_SUBLANES
    for lt in range(ext_ref.shape[1] // LANES):
        cols = slice(lt * LANES, (lt + 1) * LANES)

        def conv_block(c, carry, cols=cols):
            base = pl.multiple_of(c * CONV_ROWS, CONV_ROWS)
            acc = jnp.zeros((CONV_ROWS, LANES), F32)
            for rem in range(sub):
                part = None
                for a in range(CONV_HALO // sub + 1):
                    k = rem + sub * a - lead
                    if 0 <= k < CONV_WIDTH:
                        term = wdw_ref[k:k + 1, cols] * ext_ref[pl.ds(base + sub * a, CONV_ROWS + sub), cols]
                        part = term if part is None else part + term
                acc = acc + part[rem:rem + CONV_ROWS]
            conv_ref[pl.ds(base, CONV_ROWS), cols] = acc
            return carry

        lax.fori_loop(0, tt // CONV_ROWS, conv_block, 0)

    def chunk(c, carry):
        base = pl.multiple_of(c * CONV_CHUNK, CONV_CHUNK)
        acc = conv_ref[pl.ds(base, CONV_CHUNK), :] + bdw_ref[...]
        mu = jnp.mean(acc, axis=-1, keepdims=True)
        cen = acc - mu
        var = jnp.mean(cen * cen, axis=-1, keepdims=True)
        y = cen * lax.rsqrt(var + EPS) * lng_ref[...] + lnb_ref[...]
        act_ref[pl.ds(base, CONV_CHUNK), :] = (y * jax.nn.sigmoid(y)).astype(BF16)
        return carry

    lax.fori_loop(0, tt // CONV_CHUNK, chunk, 0)

    h = x_ref[0] + _dot(act_ref[...], w2_ref[...]) + b2_ref[...]
    h_ref[0] = h
    xn, slab = _route(h, gffn_ref[...], wcat_ref, br_ref, tri_ref, carry_ref)
    xn_ref[...] = xn
    slab_ref[...] = slab
    slabt_ref[...] = slab.T[:ROUTE_FIELDS]
    cnt_ref[...] = carry_ref[...]


def _conv_mix(u, x, w_dw, b_dw, ln_g, ln_b, w_pw2, b_pw2, g_ffn, w_rg, b_rg, w_re, b_re):
    bsz, t, d = x.shape
    tt = ROW_TILE
    nt = t // tt
    n = bsz * t
    wcat, br, tri = _router_operands(w_rg, b_rg, w_re, b_re, tt)
    wdw = jnp.zeros((CONV_HALO, d), F32).at[:CONV_WIDTH].set(w_dw)
    hb = tt // CONV_HALO
    row = lambda v: v.reshape(1, d)
    return pl.pallas_call(
        _conv_mix_kernel,
        grid=(bsz, nt),
        in_specs=[
            pl.BlockSpec((1, tt, d), lambda b, i: (b, i, 0)),
            pl.BlockSpec((1, CONV_HALO, d), lambda b, i: (b, jnp.maximum(i * hb - 1, 0), 0)),
            pl.BlockSpec((1, tt, d), lambda b, i: (b, i, 0)),
            _full((CONV_HALO, d)), _full((1, d)), _full((1, d)), _full((1, d)),
            _full((d, d)), _full((1, d)), _full((1, d)),
            _full((d, 2 * LANES)), _full((1, LANES)), _full((tt, tt)),
        ],
        out_specs=[
            pl.BlockSpec((1, tt, d), lambda b, i: (b, i, 0)),
            pl.BlockSpec((tt, d), lambda b, i: (b * nt + i, 0)),
            pl.BlockSpec((tt, LANES), lambda b, i: (b * nt + i, 0)),
            pl.BlockSpec((ROUTE_FIELDS, tt), lambda b, i: (0, b * nt + i)),
            _full((1, LANES)),
        ],
        out_shape=[
            jax.ShapeDtypeStruct((bsz, t, d), F32),
            jax.ShapeDtypeStruct((n, d), F32),
            jax.ShapeDtypeStruct((n, LANES), F32),
            jax.ShapeDtypeStruct((ROUTE_FIELDS, n), F32),
            jax.ShapeDtypeStruct((1, LANES), F32),
        ],
        scratch_shapes=[pltpu.VMEM((CONV_HALO + tt + CONV_SUBLANES, d), F32), pltpu.VMEM((tt, d), F32),
                        pltpu.VMEM((tt, d), BF16),
                        pltpu.VMEM((1, LANES), F32)],
        compiler_params=_params(2),
        name="conv_mix",
    )(u, u, x, wdw, row(b_dw), row(ln_g), row(ln_b), w_pw2.astype(BF16), row(b_pw2), row(g_ffn), wcat, br, tri)


def _invert_slot_map(dest_ref, meta_ref, asg_ref):
    n_asg = dest_ref.shape[0]
    groups = meta_ref.shape[0] // 3

    def pad_group(e, carry):
        off = meta_ref[2 * groups + e]

        def body(s, c):
            asg_ref[s] = s + off
            return c

        lax.fori_loop(meta_ref[e], meta_ref[groups + e], body, 0)
        return carry

    lax.fori_loop(0, groups, pad_group, 0)

    def real(a, carry):
        asg_ref[dest_ref[a]] = a
        return carry

    lax.fori_loop(0, n_asg, real, 0, unroll=8)


def _expert_fused_kernel(be_ref, nu_ref, dest_ref, meta_ref, xn_hbm, wg_ref, wu_ref, wd_ref, y_hbm,
                         asg_ref, xbuf, ybuf, wgb_ref, wub_ref, wdb_ref, gsem, ssem, *, n_tok):
    i = pl.program_id(0)
    nu = nu_ref[0]
    be = xbuf.shape[1]
    slot = i % 2
    other = 1 - slot

    @pl.when(i == 0)
    def _():
        _invert_slot_map(dest_ref, meta_ref, asg_ref)

    def gather_copy(blk, buf, r):
        a = asg_ref[blk * be + r]
        tok = a & (n_tok - 1) if n_tok & (n_tok - 1) == 0 else lax.rem(a, n_tok)
        return pltpu.make_async_copy(xn_hbm.at[pl.ds(tok, 1)], xbuf.at[buf, pl.ds(r, 1)], gsem.at[buf])

    def scatter_copy(blk, buf, r):
        return pltpu.make_async_copy(ybuf.at[buf, pl.ds(r, 1)], y_hbm.at[pl.ds(asg_ref[blk * be + r], 1)],
                                     ssem.at[buf])

    def for_rows(fn):
        def body(r, carry):
            fn(r)
            return carry

        lax.fori_loop(0, be, body, 0, unroll=8)

    @pl.when(i == 0)
    def _():
        for_rows(lambda r: gather_copy(0, 0, r).start())
        ybuf[1] = jnp.zeros(ybuf.shape[1:], ybuf.dtype)

    @pl.when(i < nu)
    def _():
        for_rows(lambda r: gather_copy(0, slot, r).wait())

        @pl.when((i == 0) | (be_ref[i] != be_ref[jnp.maximum(i - 1, 0)]))
        def _():
            wgb_ref[...] = wg_ref[0, 0].astype(BF16)
            wub_ref[...] = wu_ref[0, 0].astype(BF16)
            wdb_ref[...] = wd_ref[0, 0].astype(BF16)

        nxt = jnp.minimum(i + 1, nu - 1)
        prv = jnp.maximum(i - 1, 0)
        for r in range(be):
            gather_copy(nxt, other, r).start(priority=r % 2)
            scatter_copy(prv, other, r).start(priority=(r + 1) % 2)
        x = xbuf[slot].astype(BF16)
        a = _dot(x, wgb_ref[...])
        u = _dot(x, wub_ref[...])
        hid = (a * jax.nn.sigmoid(a) * u).astype(BF16)
        ybuf[slot] = _dot(hid, wdb_ref[...])
        for_rows(lambda r: scatter_copy(0, other, r).wait())

    @pl.when(i == nu)
    def _():
        for_rows(lambda r: gather_copy(0, slot, r).wait())
        for_rows(lambda r: scatter_copy(nu - 1, other, r).start())
        for_rows(lambda r: scatter_copy(0, other, r).wait())
        xbuf[slot] = jnp.zeros(xbuf.shape[1:], xbuf.dtype)

        def zero_copy(blk):
            return pltpu.make_async_copy(xbuf.at[slot], y_hbm.at[pl.ds(pl.multiple_of(blk * be, be), be)],
                                         gsem.at[slot])

        def start(blk, carry):
            zero_copy(blk).start()
            return carry

        def finish(blk, carry):
            zero_copy(blk).wait()
            return carry

        lax.fori_loop(nu, pl.num_programs(0), start, 0)
        lax.fori_loop(nu, pl.num_programs(0), finish, 0)


def _experts_fused(blk_e, n_used, dest_flat, meta, n_slots, xn, layer, w_gate, w_up, w_down):
    n, d = xn.shape
    f = w_gate.shape[3]
    be = EXPERT_ROWS
    wspec = lambda a, b: pl.BlockSpec((1, 1, a, b), lambda i, e, nu, dest, meta: (layer, e[i], 0, 0))
    return pl.pallas_call(
        functools.partial(_expert_fused_kernel, n_tok=n),
        grid_spec=pltpu.PrefetchScalarGridSpec(
            num_scalar_prefetch=4,
            grid=(n_slots // be,),
            in_specs=[pl.BlockSpec(memory_space=pl.ANY), wspec(d, f), wspec(d, f), wspec(f, d)],
            out_specs=pl.BlockSpec(memory_space=pl.ANY),
            scratch_shapes=[pltpu.SMEM((n_slots,), I32),
                            pltpu.VMEM((2, be, d), F32), pltpu.VMEM((2, be, d), F32),
                            pltpu.VMEM((d, f), BF16), pltpu.VMEM((d, f), BF16), pltpu.VMEM((f, d), BF16),
                            pltpu.SemaphoreType.DMA((2,)), pltpu.SemaphoreType.DMA((2,))],
        ),
        out_shape=jax.ShapeDtypeStruct((n_slots, d), F32),
        compiler_params=_params(1),
        name="moe_experts",
    )(blk_e, n_used, dest_flat, meta, xn, w_gate, w_up, w_down)


def _ple_kernel(h_ref, y0_ref, y1_ref, slab_ref, p_ref, gple_ref, wg_ref, wp_ref, o_ref):
    slab = slab_ref[...]
    lane = lax.broadcasted_iota(I32, slab.shape, 1)
    w0 = jnp.sum(jnp.where(lane == 2, slab, 0.0), axis=-1, keepdims=True)
    w1 = jnp.sum(jnp.where(lane == 3, slab, 0.0), axis=-1, keepdims=True)
    hm = h_ref[...] + w0 * y0_ref[...] + w1 * y1_ref[...]
    gate = jax.nn.sigmoid(_dot(_rms(hm, gple_ref[...]).astype(BF16), wg_ref[...]))
    o_ref[...] = hm + gate * _dot(p_ref[0].astype(BF16), wp_ref[...])


def _ple(h2, y, slab, layer, p3, g_ple, w_gate, w_proj):
    n, d = h2.shape
    dp = p3.shape[2]
    tm = ROW_TILE
    nt = n // tm
    row = lambda w: pl.BlockSpec((tm, w), lambda i: (i, 0))
    return pl.pallas_call(
        _ple_kernel,
        grid=(nt,),
        in_specs=[row(d), row(d), pl.BlockSpec((tm, d), lambda i: (nt + i, 0)), row(LANES),
                  pl.BlockSpec((1, tm, dp), lambda i: (layer, i, 0)), _full((1, d)), _full((d, d)), _full((dp, d))],
        out_specs=row(d),
        out_shape=jax.ShapeDtypeStruct((n, d), F32),
        compiler_params=_params(1),
        name="moe_combine_ple",
    )(h2, y, y, slab, p3, g_ple.reshape(1, d), w_gate.astype(BF16), w_proj.astype(BF16))


def _moe_ple(h2, xn, slab, slab_t, counts, layer, p3, g_ple, w_gate_e, w_up_e, w_down_e, ple_w_gate, ple_w_proj):
    n, d = h2.shape
    be = EXPERT_ROWS
    n_blocks = (2 * n) // be + N_EXPERTS
    cnt = counts[0, :N_EXPERTS].astype(I32)
    padded = (cnt + be - 1) // be * be
    ends = jnp.cumsum(padded)
    starts = ends - padded
    eid = slab_t[0:2].astype(I32)
    rank = slab_t[4:6].astype(I32)
    expert_ids = jnp.arange(N_EXPERTS, dtype=I32).reshape(N_EXPERTS, 1, 1)
    slot_base = jnp.sum(jnp.where(eid[None] == expert_ids, starts.reshape(N_EXPERTS, 1, 1), 0), axis=0)
    dest_flat = (slot_base + rank).reshape(2 * n)
    block_start = jnp.arange(n_blocks, dtype=I32) * be
    blk_e = jnp.minimum(jnp.sum((ends[None, :] <= block_start[:, None]).astype(I32), axis=1), N_EXPERTS - 1)
    n_used = (ends[-1:] // be).astype(I32)
    seen = jnp.cumsum(cnt)
    pad_lo = jnp.concatenate([starts + cnt, ends[-1:]])
    pad_hi = jnp.concatenate([ends, jnp.full((1,), n_blocks * be, I32)])
    pad_off = jnp.concatenate([2 * n - seen, jnp.zeros((1,), I32)])
    meta = jnp.concatenate([pad_lo, pad_hi, pad_off]).astype(I32)
    y = _experts_fused(blk_e, n_used, dest_flat, meta, n_blocks * be, xn, layer, w_gate_e, w_up_e, w_down_e)
    return _ple(h2, y, slab, layer, p3, g_ple, ple_w_gate, ple_w_proj)


def _rope_table_kernel(pos_ref, inv_ref, sign_ref, cos_ref, sin_ref):
    ang = pos_ref[...] * inv_ref[...]
    cos_ref[...] = jnp.cos(ang)
    sin_ref[...] = jnp.sin(ang) * sign_ref[...]


def _rope_table(positions):
    n = positions.size
    tm = ROW_TILE
    half = HEAD_DIM // 2
    lane = jnp.arange(LANES)
    inv = 1.0 / (ROPE_THETA ** ((lane % half).astype(F32) / half))
    sign = jnp.where(lane % HEAD_DIM < half, -1.0, 1.0).astype(F32)
    return pl.pallas_call(
        _rope_table_kernel,
        grid=(n // tm,),
        in_specs=[pl.BlockSpec((tm, 1), lambda i: (i, 0)), _full((1, LANES)), _full((1, LANES))],
        out_specs=[pl.BlockSpec((tm, LANES), lambda i: (i, 0))] * 2,
        out_shape=[jax.ShapeDtypeStruct((n, LANES), F32)] * 2,
        compiler_params=_params(1),
        name="rope_table",
    )(positions.astype(F32).reshape(n, 1), inv.reshape(1, LANES), sign.reshape(1, LANES))


def _head_norm_rope(z, gain, seg_ref, cos, sin):
    z2 = z * z
    hi = z2.astype(BF16)
    lo = (z2 - hi.astype(F32)).astype(BF16)
    ssq = _dot(hi, seg_ref[...]) + _dot(lo, seg_ref[...])
    zn = z * lax.rsqrt(ssq * (1.0 / HEAD_DIM) + EPS) * gain
    width = z.shape[1]
    half = HEAD_DIM // 2
    lane = lax.broadcasted_iota(I32, z.shape, 1)
    rot = jnp.where((lane & half) == 0, pltpu.roll(zn, width - half, 1), pltpu.roll(zn, half, 1))
    return zn * cos + rot * sin


def _nsa_proj_kernel(h_ref, g_ref, w_ref, cos_ref, sin_ref, qn_ref, kns_ref, knw_ref, seg_ref,
                     q_ref, kc_ref, vc_ref, ks_ref, vs_ref, kw_ref, vw_ref, gate_ref):
    d = h_ref.shape[1]
    kd = N_KV * HEAD_DIM
    xn = _rms(h_ref[...], g_ref[...]).astype(BF16)
    cos = jnp.concatenate([cos_ref[...], cos_ref[...]], axis=1)
    sin = jnp.concatenate([sin_ref[...], sin_ref[...]], axis=1)
    scale = HEAD_DIM ** -0.5 * LOG2_E

    def heads_out(ref, first, z):
        ones = jnp.ones((z.shape[0], HEAD_DIM), ref.dtype)
        for r in range(N_KV):
            zr = z[:, r * HEAD_DIM:(r + 1) * HEAD_DIM].astype(ref.dtype)
            ref[0, first + r] = zr if ref.shape[-1] == HEAD_DIM else jnp.concatenate([zr, ones], axis=1)

    for j in range(d // kd):
        z = _dot(xn, w_ref[:, j * kd:(j + 1) * kd])
        heads_out(q_ref, N_KV * j, _head_norm_rope(z, qn_ref[...], seg_ref, cos, sin) * scale)
    col = lambda c: _dot(xn, w_ref[:, d + c * kd:d + (c + 1) * kd])
    heads_out(kc_ref, 0, col(0))
    heads_out(vc_ref, 0, col(1))
    heads_out(ks_ref, 0, _head_norm_rope(col(2), kns_ref[...], seg_ref, cos, sin))
    heads_out(vs_ref, 0, col(3))
    heads_out(kw_ref, 0, _head_norm_rope(col(4), knw_ref[...], seg_ref, cos, sin))
    heads_out(vw_ref, 0, col(5))
    gate_ref[...] = jax.nn.sigmoid(_dot(xn, w_ref[:, d + 6 * kd:]))


def _nsa_proj(h, g_mix, w_in, cos_t, sin_t, q_norm, k_norm):
    bsz, t, d = h.shape
    n = bsz * t
    tm = ROW_TILE
    nt = t // tm
    kd = N_KV * HEAD_DIM
    n_heads = d // HEAD_DIM
    n_in = w_in.shape[1]
    w_pad = jnp.zeros((d, d + 6 * kd + LANES), F32).at[:, :n_in].set(w_in).astype(BF16)
    tile4 = lambda v: jnp.tile(v, kd // HEAD_DIM).reshape(1, kd)
    idx = jnp.arange(kd) // HEAD_DIM
    seg = (idx[:, None] == idx[None, :]).astype(BF16)
    head_spec = lambda nh, w=HEAD_DIM: pl.BlockSpec((1, nh, tm, w), lambda i: (i // nt, 0, i % nt, 0))
    head_shape = lambda nh, w=HEAD_DIM, dt=BF16: jax.ShapeDtypeStruct((bsz, nh, t, w), dt)
    flat_spec = lambda w: pl.BlockSpec((tm, w), lambda i: (i, 0))
    vw2 = 2 * HEAD_DIM
    return pl.pallas_call(
        _nsa_proj_kernel,
        grid=(n // tm,),
        in_specs=[flat_spec(d), _full((1, d)), _full(w_pad.shape), flat_spec(LANES), flat_spec(LANES),
                  _full((1, kd)), _full((1, kd)), _full((1, kd)), _full((kd, kd))],
        out_specs=[head_spec(n_heads), head_spec(N_KV), head_spec(N_KV), head_spec(N_KV), head_spec(N_KV, vw2),
                   head_spec(N_KV), head_spec(N_KV, vw2), flat_spec(LANES)],
        out_shape=[head_shape(n_heads), head_shape(N_KV, dt=F32), head_shape(N_KV, dt=F32),
                   head_shape(N_KV), head_shape(N_KV, vw2), head_shape(N_KV), head_shape(N_KV, vw2),
                   jax.ShapeDtypeStruct((n, LANES), F32)],
        compiler_params=_params(1),
        name="nsa_proj",
    )(h.reshape(n, d), g_mix.reshape(1, d), w_pad, cos_t, sin_t,
      tile4(q_norm), tile4(k_norm[1]), tile4(k_norm[2]), seg)


def _compress_kernel(xk_ref, xv_ref, pe_ref, w1_ref, w2_ref, kn_ref, cos_ref, sin_ref, o_ref):
    nch, dh = o_ref.shape[3], o_ref.shape[4]
    st = CMP_STRIDE
    hidden = w1_ref.shape[2]
    for which, x_ref in enumerate((xk_ref, xv_ref)):
        first = jnp.zeros((nch, hidden), F32)
        second = jnp.zeros((nch, hidden), F32)
        for tp in range(st):
            xt = x_ref[0, 0, pl.ds(tp, nch, stride=st), :]
            first = first + _dot((xt + pe_ref[which, tp:tp + 1, :]).astype(BF16),
                                 w1_ref[which, tp * dh:(tp + 1) * dh, :])
            second = second + _dot((xt + pe_ref[which, st + tp:st + tp + 1, :]).astype(BF16),
                                   w1_ref[which, (st + tp) * dh:(st + tp + 1) * dh, :])
        hid = first + pltpu.roll(second, nch - 1, 0)
        c = _dot(jax.nn.gelu(hid, approximate=True).astype(BF16), w2_ref[which])
        if which == 0:
            half = dh // 2
            cn = _rms(c, kn_ref[...])
            rot = jnp.concatenate([cn[:, half:], cn[:, :half]], axis=1)
            c = cn * cos_ref[0] + rot * sin_ref[0]
        o_ref[which, 0, 0] = c.astype(o_ref.dtype)


def _compress(kc_raw, vc_raw, pe, w1, w2, k_norm0, cos_t, sin_t):
    bsz, g, t, dh = kc_raw.shape
    st = CMP_STRIDE
    nch = t // st
    last = lambda tab: jnp.pad(tab.reshape(bsz, t, LANES)[:, CMP_BLOCK - 1::st, :dh], ((0, 0), (0, 1), (0, 0)))
    src = pl.BlockSpec((1, 1, t, dh), lambda b, gi: (b, gi, 0, 0))
    tab = pl.BlockSpec((1, nch, dh), lambda b, gi: (b, 0, 0))
    return pl.pallas_call(
        _compress_kernel,
        grid=(bsz, g),
        in_specs=[src, src, _full(pe.shape), _full(w1.shape), _full(w2.shape), _full((1, dh)), tab, tab],
        out_specs=pl.BlockSpec((2, 1, 1, nch, dh), lambda b, gi: (0, b, gi, 0, 0)),
        out_shape=jax.ShapeDtypeStruct((2, bsz, g, nch, dh), BF16),
        compiler_params=_params(2),
        name="nsa_compress",
    )(kc_raw, vc_raw, pe, w1.astype(BF16), w2.astype(BF16), k_norm0.reshape(1, dh), last(cos_t), last(sin_t))


def _attend_chunk(q, k, v, bias, m_ref, acc_ref, r_heads, tq_n):
    tk = k.shape[0]
    s = _dot_nt(q, k)
    m_all = m_ref[...]
    acc_all = acc_ref[...]
    m_out, scaled, probs = [], [], []
    for r in range(r_heads):
        rs = slice(r * tq_n, (r + 1) * tq_n)
        sr = s[rs] + bias
        m_old = m_all[rs]
        m_new = jnp.maximum(m_old, jnp.max(sr, axis=-1, keepdims=True))
        m_wide = m_new if tk == m_new.shape[1] else jnp.concatenate([m_new] * (tk // m_new.shape[1]), axis=1)
        probs.append(jnp.exp2(sr - m_wide).astype(BF16))
        m_out.append(m_new)
        scaled.append(jnp.exp2(m_old - m_new) * acc_all[rs])
    m_ref[...] = jnp.concatenate(m_out, axis=0)
    acc_ref[...] = jnp.concatenate(scaled, axis=0) + _dot(jnp.concatenate(probs, axis=0), v)


def _nsa_attn_kernel(q_ref, kc_ref, vc_ref, ks_ref, vs_ref, kw_ref, vw_ref, gate_ref, wsel_ref, e_ref, o_ref,
                     m_ref, acc_ref, og_ref, bias_ref, *, n_sel):
    g = pl.program_id(1)
    qi = pl.program_id(2)
    r_heads, tq_n, dh = q_ref.shape[1], q_ref.shape[2], q_ref.shape[3]
    rows = r_heads * tq_n
    t0 = qi * tq_n
    tk = K_TILE
    q = q_ref[0].reshape(rows, dh)
    head_rows = [slice(r * tq_n, (r + 1) * tq_n) for r in range(r_heads)]

    gates = gate_ref[0]
    gate_lane = lax.broadcasted_iota(I32, gates.shape, 1)

    def gate(r, branch):
        col = (g * r_heads + r) * 3 + branch
        return jnp.sum(jnp.where(gate_lane == col, gates, 0.0), axis=-1, keepdims=True)

    def restart():
        m_ref[...] = jnp.full(m_ref.shape, NEG, F32)
        acc_ref[...] = jnp.zeros(acc_ref.shape, F32)

    def add_branch(branch):
        acc = acc_ref[...]
        o = acc * (1.0 / pltpu.roll(acc, dh, 1))
        og_ref[...] += jnp.concatenate([gate(r, branch) * o[rs, :dh] for r, rs in enumerate(head_rows)], axis=0)

    kc = kc_ref[0, 0]
    nc = kc.shape[0]
    tq_c = t0 + lax.broadcasted_iota(I32, (tq_n, nc), 0)
    mask_c = CMP_STRIDE * lax.broadcasted_iota(I32, (tq_n, nc), 1) + (CMP_BLOCK - 1) <= tq_c
    s_c = _dot_nt(q, kc)
    imp = jnp.zeros((tq_n, nc), F32)
    probs = []
    for rs in head_rows:
        sr = jnp.where(mask_c, s_c[rs], NEG)
        e = jnp.where(mask_c, jnp.exp2(sr - jnp.max(sr, axis=-1, keepdims=True)), 0.0)
        l = jnp.sum(e, axis=-1, keepdims=True)
        p = e * jnp.where(l > 0.0, 1.0 / l, 0.0)
        imp = imp + p
        probs.append(p.astype(BF16))
    o_c = _dot(jnp.concatenate(probs, axis=0), vc_ref[0, 0])
    og_ref[...] = jnp.concatenate([gate(r, 0) * o_c[rs] for r, rs in enumerate(head_rows)], axis=0)

    imp_hi = imp.astype(BF16)
    imp_lo = (imp - imp_hi.astype(F32)).astype(BF16)
    p_slc = _dot_nt(wsel_ref[...], imp_hi) + _dot_nt(wsel_ref[...], imp_lo)
    nsb = p_slc.shape[0]
    blk = lax.broadcasted_iota(I32, (nsb, tq_n), 0)
    tq_t = t0 + lax.broadcasted_iota(I32, (nsb, tq_n), 1)
    cur = tq_t // SLC_BLOCK
    forced = (blk == 0) | (blk == cur) | (blk == cur - 1)
    score = jnp.where(forced, jnp.inf, jnp.where(blk * SLC_BLOCK <= tq_t, p_slc, -jnp.inf))
    sub = CONV_SUBLANES
    groups = [slice(g0, g0 + sub) for g0 in range(0, nsb, sub)]
    cnt = [jnp.zeros((sub, tq_n), F32) for _ in groups]
    for i in range(nsb):
        ri = score[i:i + 1, :]
        for gi, rows_g in enumerate(groups):
            sg = score[rows_g]
            if rows_g.start > i:
                beats = ri >= sg
            elif rows_g.stop <= i:
                beats = ri > sg
            else:
                idx = rows_g.start + lax.broadcasted_iota(I32, (sub, tq_n), 0)
                beats = (ri > sg) | ((ri == sg) & (idx > i))
            cnt[gi] = cnt[gi] + jnp.where(beats, 1.0, 0.0)
    dropped = jnp.where(jnp.concatenate(cnt, axis=0) < n_sel, 0.0, 1.0)
    dropped = jnp.concatenate([dropped, jnp.zeros((LANES - nsb, tq_n), F32)], axis=0).T.astype(BF16)

    kwn = WINDOW + tq_n
    start = pl.multiple_of(jnp.maximum(t0 - WINDOW, 0), tq_n)
    kpos_w = start + lax.broadcasted_iota(I32, (tq_n, kwn), 1)
    tq_w = t0 + lax.broadcasted_iota(I32, (tq_n, kwn), 0)
    bias_w = jnp.where((kpos_w <= tq_w) & (kpos_w > tq_w - WINDOW), 0.0, NEG)
    s_w = _dot_nt(q, kw_ref[0, 0, pl.ds(start, kwn), :])
    probs = []
    for rs in head_rows:
        sr = s_w[rs] + bias_w
        probs.append(jnp.exp2(sr - jnp.max(sr, axis=-1, keepdims=True)).astype(BF16))
    acc_w = _dot(jnp.concatenate(probs, axis=0), vw_ref[0, 0, pl.ds(start, kwn), :])
    o_w = acc_w * (1.0 / pltpu.roll(acc_w, dh, 1))
    og_ref[...] += jnp.concatenate([gate(r, 2) * o_w[rs, :dh] for r, rs in enumerate(head_rows)], axis=0)

    t_keys = e_ref.shape[1]
    kpos = lax.broadcasted_iota(I32, (tq_n, t_keys), 1)
    tq_s = t0 + lax.broadcasted_iota(I32, (tq_n, t_keys), 0)
    bias_all = jnp.where(kpos <= tq_s, _dot(dropped, e_ref[...]), NEG)
    for c in range(t_keys // tk):
        bias_ref[c] = bias_all[:, c * tk:(c + 1) * tk]
    restart()

    def sel_step(c, carry):
        k0 = pl.multiple_of(c * tk, tk)
        _attend_chunk(q, ks_ref[0, 0, pl.ds(k0, tk), :], vs_ref[0, 0, pl.ds(k0, tk), :], bias_ref[c],
                      m_ref, acc_ref, r_heads, tq_n)
        return carry

    lax.fori_loop(0, (t0 + tq_n + tk - 1) // tk, sel_step, 0)
    add_branch(1)

    for r, rs in enumerate(head_rows):
        o_ref[0, :, r * dh:(r + 1) * dh] = og_ref[rs].astype(o_ref.dtype)


def _nsa_attn(q, kc, vc, ks, vs, kw, vw, gate, n_sel):
    bsz, n_heads, t, dh = q.shape
    g = N_KV
    r_heads = n_heads // g
    tq = Q_TILE
    nch = kc.shape[2]
    a = SLC_BLOCK // CMP_STRIDE
    bb = CMP_BLOCK // CMP_STRIDE
    nsb = t // SLC_BLOCK
    j = jnp.arange(nsb)[:, None]
    c = jnp.arange(nch)[None, :]
    wsel = sum(((c == a * j + m + n_) & (c < nch - 1)).astype(F32)
               for m in range(a) for n_ in range(bb)).astype(BF16)
    tk = K_TILE
    key_blk = (jnp.arange(t) // SLC_BLOCK).reshape(1, t)
    e = jnp.where(key_blk == jnp.arange(LANES).reshape(LANES, 1), NEG, 0.0).astype(BF16)
    rows = r_heads * tq
    kv_spec = lambda n_rows, w=dh: pl.BlockSpec((1, 1, n_rows, w), lambda b, gi, qi: (b, gi, 0, 0))
    return pl.pallas_call(
        functools.partial(_nsa_attn_kernel, n_sel=n_sel),
        grid=(bsz, g, t // tq),
        in_specs=[
            pl.BlockSpec((1, r_heads, tq, dh), lambda b, gi, qi: (b, gi, qi, 0)),
            kv_spec(nch), kv_spec(nch), kv_spec(t), kv_spec(t, 2 * dh), kv_spec(t), kv_spec(t, 2 * dh),
            pl.BlockSpec((1, tq, LANES), lambda b, gi, qi: (b, qi, 0)),
            pl.BlockSpec((nsb, nch), lambda b, gi, qi: (0, 0)),
            pl.BlockSpec(e.shape, lambda b, gi, qi: (0, 0)),
        ],
        out_specs=pl.BlockSpec((1, tq, r_heads * dh), lambda b, gi, qi: (b, qi, gi)),
        out_shape=jax.ShapeDtypeStruct((bsz, t, n_heads * dh), BF16),
        scratch_shapes=[pltpu.VMEM((rows, 2 * dh), F32), pltpu.VMEM((rows, 2 * dh), F32),
                        pltpu.VMEM((rows, dh), F32), pltpu.VMEM((t // tk, tq, tk), F32)],
        compiler_params=_params(3),
        name="nsa_attention",
    )(q, kc, vc, ks, vs, kw, vw, gate, wsel, e)


def _out_mix_kernel(o_ref, w_ref, h_ref, gffn_ref, wcat_ref, br_ref, tri_ref,
                    h2_ref, xn_ref, slab_ref, slabt_ref, cnt_ref, carry_ref):
    @pl.when(pl.program_id(0) == 0)
    def _():
        carry_ref[...] = jnp.zeros_like(carry_ref)

    h = h_ref[...] + _dot(o_ref[...], w_ref[...])
    h2_ref[...] = h
    xn, slab = _route(h, gffn_ref[...], wcat_ref, br_ref, tri_ref, carry_ref)
    xn_ref[...] = xn
    slab_ref[...] = slab
    slabt_ref[...] = slab.T[:ROUTE_FIELDS]
    cnt_ref[...] = carry_ref[...]


def _out_mix(o2, w_out, h2, g_ffn, w_rg, b_rg, w_re, b_re):
    n, d = h2.shape
    tm = ROW_TILE
    wcat, br, tri = _router_operands(w_rg, b_rg, w_re, b_re, tm)
    row = pl.BlockSpec((tm, d), lambda i: (i, 0))
    return pl.pallas_call(
        _out_mix_kernel,
        grid=(n // tm,),
        in_specs=[row, _full((d, d)), row, _full((1, d)), _full((d, 2 * LANES)), _full((1, LANES)), _full((tm, tm))],
        out_specs=[row, row, pl.BlockSpec((tm, LANES), lambda i: (i, 0)),
                   pl.BlockSpec((ROUTE_FIELDS, tm), lambda i: (0, i)), _full((1, LANES))],
        out_shape=[jax.ShapeDtypeStruct((n, d), F32), jax.ShapeDtypeStruct((n, d), F32),
                   jax.ShapeDtypeStruct((n, LANES), F32), jax.ShapeDtypeStruct((ROUTE_FIELDS, n), F32),
                   jax.ShapeDtypeStruct((1, LANES), F32)],
        scratch_shapes=[pltpu.VMEM((1, LANES), F32)],
        compiler_params=_params(1),
        name="nsa_out_mix",
    )(o2, w_out.astype(BF16), h2, g_ffn.reshape(1, d), wcat, br, tri)


def _nsa_layer(h, positions, g_mix, w_in, q_norm, k_norm, cmp_pe, cmp_w1, cmp_w2):
    bsz, t, d = h.shape
    cos_t, sin_t = _rope_table(positions)
    q, kc_raw, vc_raw, ks, vs, kw, vw, gate = _nsa_proj(h, g_mix, w_in, cos_t, sin_t, q_norm, k_norm)
    cmp = _compress(kc_raw, vc_raw, cmp_pe, cmp_w1, cmp_w2, k_norm[0], cos_t, sin_t)
    n_sel = min(N_SEL, t // SLC_BLOCK)
    return _nsa_attn(q, cmp[0], cmp[1], ks, vs, kw, vw, gate.reshape(bsz, t, LANES), n_sel)


def kernel(x, p, positions, g_mix, g_ffn, g_ple, conv_w_pw1, conv_b_pw1, conv_w_dw, conv_b_dw, conv_ln_g, conv_ln_b, conv_w_pw2, conv_b_pw2, nsa_w_in, nsa_q_norm, nsa_k_norm, nsa_cmp_pe, nsa_cmp_w1, nsa_cmp_w2, nsa_w_out, moe_w_rg, moe_b_rg, moe_w_re, moe_b_re, moe_w_gate, moe_w_up, moe_w_down, ple_w_proj, ple_w_gate):
    bsz, t, d = x.shape
    n = bsz * t
    depth = p.shape[0]
    h = x
    p3 = p.reshape(depth, n, p.shape[-1])
    for i in range(depth):
        j = i // 2
        route_w = (g_ffn[i], moe_w_rg[i], moe_b_rg[i], moe_w_re[i], moe_b_re[i])
        if i % 2 == 0:
            u = _pw1_glu(h.reshape(n, d), g_mix[i], conv_w_pw1[j], conv_b_pw1[j])
            h, xn, slab, slab_t, counts = _conv_mix(u.reshape(bsz, t, d), h, conv_w_dw[j], conv_b_dw[j],
                                                    conv_ln_g[j], conv_ln_b[j], conv_w_pw2[j], conv_b_pw2[j], *route_w)
        else:
            o = _nsa_layer(h, positions, g_mix[i], nsa_w_in[j], nsa_q_norm[j], nsa_k_norm[j],
                           nsa_cmp_pe[j], nsa_cmp_w1[j], nsa_cmp_w2[j])
            h, xn, slab, slab_t, counts = _out_mix(o.reshape(n, d), nsa_w_out[j], h.reshape(n, d), *route_w)
        h = _moe_ple(h.reshape(n, d), xn, slab, slab_t, counts, i, p3, g_ple[i],
                     moe_w_gate, moe_w_up, moe_w_down, ple_w_gate[i], ple_w_proj[i]).reshape(bsz, t, d)
    return h
```

```python
import functools

import jax
import jax.numpy as jnp
from jax import lax
from jax.experimental import pallas as pl
from jax.experimental.pallas import tpu as pltpu

F32 = jnp.float32
BF16 = jnp.bfloat16
I32 = jnp.int32

LANES = 128
VMEM_LIMIT = 56 * 1024 * 1024

EPS = 1e-6
NEG = -1e30
LOG2_E = 1.4426950408889634

CONV_WIDTH = 31
HEAD_DIM = 64
N_KV = 4
CMP_BLOCK = 32
CMP_STRIDE = 16
SLC_BLOCK = 64
N_SEL = 16
WINDOW = 512
ROPE_THETA = 10000.0
N_GROUPS = 4
EXPERTS_PER_GROUP = 8
N_EXPERTS = N_GROUPS * EXPERTS_PER_GROUP

ROW_TILE = 512
CONV_CHUNK = 64
CONV_ROWS = 128
CONV_SUBLANES = 8
CONV_HALO = 32
EXPERT_ROWS = 256
Q_TILE = 256
K_TILE = 512
ROUTE_FIELDS = 8


def _params(n_grid):
    return pltpu.CompilerParams(dimension_semantics=("arbitrary",) * n_grid,
                                vmem_limit_bytes=VMEM_LIMIT)


def _dot(a, b):
    return jnp.dot(a, b, preferred_element_type=F32)


def _dot_nt(a, b):
    return lax.dot_general(a, b, (((1,), (1,)), ((), ())), preferred_element_type=F32)


def _rms(x, g):
    return x * lax.rsqrt(jnp.mean(x * x, axis=-1, keepdims=True) + EPS) * g


def _full(shape):
    n = len(shape)
    return pl.BlockSpec(shape, lambda *_: (0,) * n)


def _route(h, g_ffn, wcat_ref, br_ref, tri_ref, carry_ref):
    xn = _rms(h, g_ffn)
    hi = xn.astype(BF16)
    lo = (xn - hi.astype(F32)).astype(BF16)
    r1 = _dot(hi, wcat_ref[...])
    r2 = _dot(lo, wcat_ref[:, :LANES])
    logits = r1[:, :LANES] + r1[:, LANES:] + r2 + br_ref[...]

    lane = lax.broadcasted_iota(I32, logits.shape, 1).astype(F32)
    lg = jnp.where(lane < N_GROUPS, logits, -jnp.inf)
    m = jnp.max(lg, axis=-1, keepdims=True)
    grp = jnp.min(jnp.where(lg == m, lane, float(LANES)), axis=-1, keepdims=True)
    pg_sel = 1.0 / jnp.sum(jnp.exp(lg - m), axis=-1, keepdims=True)

    lo_e = N_GROUPS + EXPERTS_PER_GROUP * grp
    le = jnp.where((lane >= lo_e) & (lane < lo_e + EXPERTS_PER_GROUP), logits, -jnp.inf)
    v1 = jnp.max(le, axis=-1, keepdims=True)
    i1 = jnp.min(jnp.where(le == v1, lane, float(LANES)), axis=-1, keepdims=True)
    le2 = jnp.where(lane == i1, -jnp.inf, le)
    v2 = jnp.max(le2, axis=-1, keepdims=True)
    i2 = jnp.min(jnp.where(le2 == v2, lane, float(LANES)), axis=-1, keepdims=True)
    e21 = jnp.exp(v2 - v1)
    w1 = pg_sel / (1.0 + e21)
    w2 = pg_sel * e21 / (1.0 + e21)
    e1 = i1 - N_GROUPS
    e2 = i2 - N_GROUPS

    oh1 = jnp.where(lane == e1, 1.0, 0.0)
    oh2 = jnp.where(lane == e2, 1.0, 0.0)
    c1 = _dot(tri_ref[...], oh1.astype(BF16))
    c2 = _dot(tri_ref[...], oh2.astype(BF16))
    carry = carry_ref[...]
    tot1 = jnp.sum(oh1, axis=0, keepdims=True)
    tot2 = jnp.sum(oh2, axis=0, keepdims=True)
    rank1 = jnp.sum(oh1 * (c1 + carry), axis=-1, keepdims=True)
    rank2 = jnp.sum(oh2 * (c2 + carry + tot1), axis=-1, keepdims=True)
    carry_ref[...] = carry + tot1 + tot2

    slab = jnp.where(lane == 0, e1, jnp.where(lane == 1, e2, jnp.where(lane == 2, w1, jnp.where(
        lane == 3, w2, jnp.where(lane == 4, rank1, jnp.where(lane == 5, rank2, 0.0))))))
    return xn, slab


def _router_operands(w_rg, b_rg, w_re, b_re, tm):
    d = w_rg.shape[0]
    w = jnp.zeros((d, LANES), F32).at[:, :N_GROUPS].set(w_rg).at[:, N_GROUPS:N_GROUPS + N_EXPERTS].set(w_re)
    w_hi = w.astype(BF16)
    w_lo = (w - w_hi.astype(F32)).astype(BF16)
    wcat = jnp.concatenate([w_hi, w_lo], axis=1)
    br = jnp.zeros((1, LANES), F32).at[0, :N_GROUPS].set(b_rg).at[0, N_GROUPS:N_GROUPS + N_EXPERTS].set(b_re)
    r = lax.broadcasted_iota(I32, (tm, tm), 0)
    c = lax.broadcasted_iota(I32, (tm, tm), 1)
    tri = (c < r).astype(BF16)
    return wcat, br, tri


def _pw1_glu_kernel(x_ref, g_ref, w_ref, b_ref, o_ref, *, chunk):
    xn = _rms(x_ref[...], g_ref[...]).astype(BF16)
    d = o_ref.shape[1]
    for j in range(d // chunk):
        sa = slice(j * chunk, (j + 1) * chunk)
        sg = slice(d + j * chunk, d + (j + 1) * chunk)
        a = _dot(xn, w_ref[:, sa]) + b_ref[:, sa]
        g = _dot(xn, w_ref[:, sg]) + b_ref[:, sg]
        o_ref[:, sa] = a * jax.nn.sigmoid(g)


def _pw1_glu(x2, g, w, b):
    n, d = x2.shape
    tm = ROW_TILE
    return pl.pallas_call(
        functools.partial(_pw1_glu_kernel, chunk=512),
        grid=(n // tm,),
        in_specs=[pl.BlockSpec((tm, d), lambda i: (i, 0)), _full((1, d)), _full((d, 2 * d)), _full((1, 2 * d))],
        out_specs=pl.BlockSpec((tm, d), lambda i: (i, 0)),
        out_shape=jax.ShapeDtypeStruct((n, d), F32),
        compiler_params=_params(1),
        name="pw1_glu",
    )(x2, g.reshape(1, d), w.astype(BF16), b.reshape(1, 2 * d))


def _conv_mix_kernel(u_ref, halo_ref, x_ref, wdw_ref, bdw_ref, lng_ref, lnb_ref, w2_ref, b2_ref,
                     gffn_ref, wcat_ref, br_ref, tri_ref,
                     h_ref, xn_ref, slab_ref, slabt_ref, cnt_ref,
                     ext_ref, conv_ref, act_ref, carry_ref):
    b = pl.program_id(0)
    i = pl.program_id(1)
    tt = u_ref.shape[1]

    @pl.when((b == 0) & (i == 0))
    def _():
        carry_ref[...] = jnp.zeros_like(carry_ref)

    halo = halo_ref[0]
    ext_ref[0:CONV_HALO, :] = jnp.where(i == 0, jnp.zeros_like(halo), halo)
    ext_ref[CONV_HALO:CONV_HALO + tt, :] = u_ref[0]
    ext_ref[CONV_HALO + tt:, :] = jnp.zeros((CONV_SUBLANES, ext_ref.shape[1]), F32)
    lead = CONV_HALO - (CONV_WIDTH - 1)

    sub = CONV_SUBLANES
    for lt in range(ext_ref.shape[1] // LANES):
        cols = slice(lt * LANES, (lt + 1) * LANES)

        def conv_block(c, carry, cols=cols):
            base = pl.multiple_of(c * CONV_ROWS, CONV_ROWS)
            acc = jnp.zeros((CONV_ROWS, LANES), F32)
            for rem in range(sub):
                part = None
                for a in range(CONV_HALO // sub + 1):
                    k = rem + sub * a - lead
                    if 0 <= k < CONV_WIDTH:
                        term = wdw_ref[k:k + 1, cols] * ext_ref[pl.ds(base + sub * a, CONV_ROWS + sub), cols]
                        part = term if part is None else part + term
                acc = acc + part[rem:rem + CONV_ROWS]
            conv_ref[pl.ds(base, CONV_ROWS), cols] = acc
            return carry

        lax.fori_loop(0, tt // CONV_ROWS, conv_block, 0)

    def chunk(c, carry):
        base = pl.multiple_of(c * CONV_CHUNK, CONV_CHUNK)
        acc = conv_ref[pl.ds(base, CONV_CHUNK), :] + bdw_ref[...]
        mu = jnp.mean(acc, axis=-1, keepdims=True)
        cen = acc - mu
        var = jnp.mean(cen * cen, axis=-1, keepdims=True)
        y = cen * lax.rsqrt(var + EPS) * lng_ref[...] + lnb_ref[...]
        act_ref[pl.ds(base, CONV_CHUNK), :] = (y * jax.nn.sigmoid(y)).astype(BF16)
        return carry

    lax.fori_loop(0, tt // CONV_CHUNK, chunk, 0)

    h = x_ref[0] + _dot(act_ref[...], w2_ref[...]) + b2_ref[...]
    h_ref[0] = h
    xn, slab = _route(h, gffn_ref[...], wcat_ref, br_ref, tri_ref, carry_ref)
    xn_ref[...] = xn
    slab_ref[...] = slab
    slabt_ref[...] = slab.T[:ROUTE_FIELDS]
    cnt_ref[...] = carry_ref[...]


def _conv_mix(u, x, w_dw, b_dw, ln_g, ln_b, w_pw2, b_pw2, g_ffn, w_rg, b_rg, w_re, b_re):
    bsz, t, d = x.shape
    tt = ROW_TILE
    nt = t // tt
    n = bsz * t
    wcat, br, tri = _router_operands(w_rg, b_rg, w_re, b_re, tt)
    wdw = jnp.zeros((CONV_HALO, d), F32).at[:CONV_WIDTH].set(w_dw)
    hb = tt // CONV_HALO
    row = lambda v: v.reshape(1, d)
    return pl.pallas_call(
        _conv_mix_kernel,
        grid=(bsz, nt),
        in_specs=[
            pl.BlockSpec((1, tt, d), lambda b, i: (b, i, 0)),
            pl.BlockSpec((1, CONV_HALO, d), lambda b, i: (b, jnp.maximum(i * hb - 1, 0), 0)),
            pl.BlockSpec((1, tt, d), lambda b, i: (b, i, 0)),
            _full((CONV_HALO, d)), _full((1, d)), _full((1, d)), _full((1, d)),
            _full((d, d)), _full((1, d)), _full((1, d)),
            _full((d, 2 * LANES)), _full((1, LANES)), _full((tt, tt)),
        ],
        out_specs=[
            pl.BlockSpec((1, tt, d), lambda b, i: (b, i, 0)),
            pl.BlockSpec((tt, d), lambda b, i: (b * nt + i, 0)),
            pl.BlockSpec((tt, LANES), lambda b, i: (b * nt + i, 0)),
            pl.BlockSpec((ROUTE_FIELDS, tt), lambda b, i: (0, b * nt + i)),
            _full((1, LANES)),
        ],
        out_shape=[
            jax.ShapeDtypeStruct((bsz, t, d), F32),
            jax.ShapeDtypeStruct((n, d), F32),
            jax.ShapeDtypeStruct((n, LANES), F32),
            jax.ShapeDtypeStruct((ROUTE_FIELDS, n), F32),
            jax.ShapeDtypeStruct((1, LANES), F32),
        ],
        scratch_shapes=[pltpu.VMEM((CONV_HALO + tt + CONV_SUBLANES, d), F32), pltpu.VMEM((tt, d), F32),
                        pltpu.VMEM((tt, d), BF16),
                        pltpu.VMEM((1, LANES), F32)],
        compiler_params=_params(2),
        name="conv_mix",
    )(u, u, x, wdw, row(b_dw), row(ln_g), row(ln_b), w_pw2.astype(BF16), row(b_pw2), row(g_ffn), wcat, br, tri)


def _invert_slot_map(dest_ref, meta_ref, asg_ref):
    n_asg = dest_ref.shape[0]
    groups = meta_ref.shape[0] // 3

    def pad_group(e, carry):
        off = meta_ref[2 * groups + e]

        def body(s, c):
            asg_ref[s] = s + off
            return c

        lax.fori_loop(meta_ref[e], meta_ref[groups + e], body, 0)
        return carry

    lax.fori_loop(0, groups, pad_group, 0)

    def real(a, carry):
        asg_ref[dest_ref[a]] = a
        return carry

    lax.fori_loop(0, n_asg, real, 0, unroll=8)


def _expert_fused_kernel(be_ref, nu_ref, dest_ref, meta_ref, xn_hbm, wg_ref, wu_ref, wd_ref, y_hbm,
                         asg_ref, xbuf, ybuf, wgb_ref, wub_ref, wdb_ref, gsem, ssem, *, n_tok):
    i = pl.program_id(0)
    nu = nu_ref[0]
    be = xbuf.shape[1]
    slot = i % 2
    other = 1 - slot

    @pl.when(i == 0)
    def _():
        _invert_slot_map(dest_ref, meta_ref, asg_ref)

    def gather_copy(blk, buf, r):
        a = asg_ref[blk * be + r]
        tok = a & (n_tok - 1) if n_tok & (n_tok - 1) == 0 else lax.rem(a, n_tok)
        return pltpu.make_async_copy(xn_hbm.at[pl.ds(tok, 1)], xbuf.at[buf, pl.ds(r, 1)], gsem.at[buf])

    def scatter_copy(blk, buf, r):
        return pltpu.make_async_copy(ybuf.at[buf, pl.ds(r, 1)], y_hbm.at[pl.ds(asg_ref[blk * be + r], 1)],
                                     ssem.at[buf])

    def for_rows(fn):
        def body(r, carry):
            fn(r)
            return carry

        lax.fori_loop(0, be, body, 0, unroll=8)

    @pl.when(i == 0)
    def _():
        for_rows(lambda r: gather_copy(0, 0, r).start())
        ybuf[1] = jnp.zeros(ybuf.shape[1:], ybuf.dtype)

    def expert_step(cur, oth):
        for_rows(lambda r: gather_copy(0, cur, r).wait())

        @pl.when((i == 0) | (be_ref[i] != be_ref[jnp.maximum(i - 1, 0)]))
        def _():
            wgb_ref[...] = wg_ref[0, 0].astype(BF16)
            wub_ref[...] = wu_ref[0, 0].astype(BF16)
            wdb_ref[...] = wd_ref[0, 0].astype(BF16)

        nxt = jnp.minimum(i + 1, nu - 1)
        prv = jnp.maximum(i - 1, 0)
        for r in range(be):
            gather_copy(nxt, oth, r).start(priority=r % 2)
            scatter_copy(prv, oth, r).start(priority=(r + 1) % 2)
        x = xbuf[cur].astype(BF16)
        a = _dot(x, wgb_ref[...])
        u = _dot(x, wub_ref[...])
        hid = (a * jax.nn.sigmoid(a) * u).astype(BF16)
        ybuf[cur] = _dot(hid, wdb_ref[...])
        for_rows(lambda r: scatter_copy(0, oth, r).wait())

    for half in (0, 1):
        pl.when((i < nu) & (slot == half))(functools.partial(expert_step, half, 1 - half))

    @pl.when(i == nu)
    def _():
        for_rows(lambda r: gather_copy(0, slot, r).wait())
        for_rows(lambda r: scatter_copy(nu - 1, other, r).start())
        for_rows(lambda r: scatter_copy(0, other, r).wait())
        xbuf[slot] = jnp.zeros(xbuf.shape[1:], xbuf.dtype)

        def zero_copy(blk):
            return pltpu.make_async_copy(xbuf.at[slot], y_hbm.at[pl.ds(pl.multiple_of(blk * be, be), be)],
                                         gsem.at[slot])

        def start(blk, carry):
            zero_copy(blk).start()
            return carry

        def finish(blk, carry):
            zero_copy(blk).wait()
            return carry

        lax.fori_loop(nu, pl.num_programs(0), start, 0)
        lax.fori_loop(nu, pl.num_programs(0), finish, 0)


def _experts_fused(blk_e, n_used, dest_flat, meta, n_slots, xn, layer, w_gate, w_up, w_down):
    n, d = xn.shape
    f = w_gate.shape[3]
    be = EXPERT_ROWS
    wspec = lambda a, b: pl.BlockSpec((1, 1, a, b), lambda i, e, nu, dest, meta: (layer, e[i], 0, 0))
    return pl.pallas_call(
        functools.partial(_expert_fused_kernel, n_tok=n),
        grid_spec=pltpu.PrefetchScalarGridSpec(
            num_scalar_prefetch=4,
            grid=(n_slots // be,),
            in_specs=[pl.BlockSpec(memory_space=pl.ANY), wspec(d, f), wspec(d, f), wspec(f, d)],
            out_specs=pl.BlockSpec(memory_space=pl.ANY),
            scratch_shapes=[pltpu.SMEM((n_slots,), I32),
                            pltpu.VMEM((2, be, d), F32), pltpu.VMEM((2, be, d), F32),
                            pltpu.VMEM((d, f), BF16), pltpu.VMEM((d, f), BF16), pltpu.VMEM((f, d), BF16),
                            pltpu.SemaphoreType.DMA((2,)), pltpu.SemaphoreType.DMA((2,))],
        ),
        out_shape=jax.ShapeDtypeStruct((n_slots, d), F32),
        compiler_params=_params(1),
        name="moe_experts",
    )(blk_e, n_used, dest_flat, meta, xn, w_gate, w_up, w_down)


def _ple_kernel(h_ref, y0_ref, y1_ref, slab_ref, p_ref, gple_ref, wg_ref, wp_ref, o_ref):
    slab = slab_ref[...]
    lane = lax.broadcasted_iota(I32, slab.shape, 1)
    w0 = jnp.sum(jnp.where(lane == 2, slab, 0.0), axis=-1, keepdims=True)
    w1 = jnp.sum(jnp.where(lane == 3, slab, 0.0), axis=-1, keepdims=True)
    hm = h_ref[...] + w0 * y0_ref[...] + w1 * y1_ref[...]
    gate = jax.nn.sigmoid(_dot(_rms(hm, gple_ref[...]).astype(BF16), wg_ref[...]))
    o_ref[...] = hm + gate * _dot(p_ref[0].astype(BF16), wp_ref[...])


def _ple(h2, y, slab, layer, p3, g_ple, w_gate, w_proj):
    n, d = h2.shape
    dp = p3.shape[2]
    tm = ROW_TILE
    nt = n // tm
    row = lambda w: pl.BlockSpec((tm, w), lambda i: (i, 0))
    return pl.pallas_call(
        _ple_kernel,
        grid=(nt,),
        in_specs=[row(d), row(d), pl.BlockSpec((tm, d), lambda i: (nt + i, 0)), row(LANES),
                  pl.BlockSpec((1, tm, dp), lambda i: (layer, i, 0)), _full((1, d)), _full((d, d)), _full((dp, d))],
        out_specs=row(d),
        out_shape=jax.ShapeDtypeStruct((n, d), F32),
        compiler_params=_params(1),
        name="moe_combine_ple",
    )(h2, y, y, slab, p3, g_ple.reshape(1, d), w_gate.astype(BF16), w_proj.astype(BF16))


def _moe_ple(h2, xn, slab, slab_t, counts, layer, p3, g_ple, w_gate_e, w_up_e, w_down_e, ple_w_gate, ple_w_proj):
    n, d = h2.shape
    be = EXPERT_ROWS
    n_blocks = (2 * n) // be + N_EXPERTS
    cnt = counts[0, :N_EXPERTS].astype(I32)
    padded = (cnt + be - 1) // be * be
    ends = jnp.cumsum(padded)
    starts = ends - padded
    eid = slab_t[0:2].astype(I32)
    rank = slab_t[4:6].astype(I32)
    expert_ids = jnp.arange(N_EXPERTS, dtype=I32).reshape(N_EXPERTS, 1, 1)
    slot_base = jnp.sum(jnp.where(eid[None] == expert_ids, starts.reshape(N_EXPERTS, 1, 1), 0), axis=0)
    dest_flat = (slot_base + rank).reshape(2 * n)
    block_start = jnp.arange(n_blocks, dtype=I32) * be
    blk_e = jnp.minimum(jnp.sum((ends[None, :] <= block_start[:, None]).astype(I32), axis=1), N_EXPERTS - 1)
    n_used = (ends[-1:] // be).astype(I32)
    seen = jnp.cumsum(cnt)
    pad_lo = jnp.concatenate([starts + cnt, ends[-1:]])
    pad_hi = jnp.concatenate([ends, jnp.full((1,), n_blocks * be, I32)])
    pad_off = jnp.concatenate([2 * n - seen, jnp.zeros((1,), I32)])
    meta = jnp.concatenate([pad_lo, pad_hi, pad_off]).astype(I32)
    y = _experts_fused(blk_e, n_used, dest_flat, meta, n_blocks * be, xn, layer, w_gate_e, w_up_e, w_down_e)
    return _ple(h2, y, slab, layer, p3, g_ple, ple_w_gate, ple_w_proj)


def _rope_table_kernel(pos_ref, inv_ref, sign_ref, cos_ref, sin_ref):
    ang = pos_ref[...] * inv_ref[...]
    cos_ref[...] = jnp.cos(ang)
    sin_ref[...] = jnp.sin(ang) * sign_ref[...]


def _rope_table(positions):
    n = positions.size
    tm = ROW_TILE
    half = HEAD_DIM // 2
    lane = jnp.arange(LANES)
    inv = 1.0 / (ROPE_THETA ** ((lane % half).astype(F32) / half))
    sign = jnp.where(lane % HEAD_DIM < half, -1.0, 1.0).astype(F32)
    return pl.pallas_call(
        _rope_table_kernel,
        grid=(n // tm,),
        in_specs=[pl.BlockSpec((tm, 1), lambda i: (i, 0)), _full((1, LANES)), _full((1, LANES))],
        out_specs=[pl.BlockSpec((tm, LANES), lambda i: (i, 0))] * 2,
        out_shape=[jax.ShapeDtypeStruct((n, LANES), F32)] * 2,
        compiler_params=_params(1),
        name="rope_table",
    )(positions.astype(F32).reshape(n, 1), inv.reshape(1, LANES), sign.reshape(1, LANES))


def _head_norm_rope(z, gain, seg_ref, cos, sin):
    z2 = z * z
    hi = z2.astype(BF16)
    lo = (z2 - hi.astype(F32)).astype(BF16)
    ssq = _dot(hi, seg_ref[...]) + _dot(lo, seg_ref[...])
    zn = z * lax.rsqrt(ssq * (1.0 / HEAD_DIM) + EPS) * gain
    width = z.shape[1]
    half = HEAD_DIM // 2
    lane = lax.broadcasted_iota(I32, z.shape, 1)
    rot = jnp.where((lane & half) == 0, pltpu.roll(zn, width - half, 1), pltpu.roll(zn, half, 1))
    return zn * cos + rot * sin


def _nsa_proj_kernel(h_ref, g_ref, w_ref, cos_ref, sin_ref, qn_ref, kns_ref, knw_ref, seg_ref,
                     q_ref, kc_ref, vc_ref, ks_ref, vs_ref, kw_ref, vw_ref, gate_ref):
    d = h_ref.shape[1]
    kd = N_KV * HEAD_DIM
    xn = _rms(h_ref[...], g_ref[...]).astype(BF16)
    cos = jnp.concatenate([cos_ref[...], cos_ref[...]], axis=1)
    sin = jnp.concatenate([sin_ref[...], sin_ref[...]], axis=1)
    scale = HEAD_DIM ** -0.5 * LOG2_E

    def heads_out(ref, first, z):
        ones = jnp.ones((z.shape[0], HEAD_DIM), ref.dtype)
        for r in range(N_KV):
            zr = z[:, r * HEAD_DIM:(r + 1) * HEAD_DIM].astype(ref.dtype)
            ref[0, first + r] = zr if ref.shape[-1] == HEAD_DIM else jnp.concatenate([zr, ones], axis=1)

    for j in range(d // kd):
        z = _dot(xn, w_ref[:, j * kd:(j + 1) * kd])
        heads_out(q_ref, N_KV * j, _head_norm_rope(z, qn_ref[...], seg_ref, cos, sin) * scale)
    col = lambda c: _dot(xn, w_ref[:, d + c * kd:d + (c + 1) * kd])
    heads_out(kc_ref, 0, col(0))
    heads_out(vc_ref, 0, col(1))
    heads_out(ks_ref, 0, _head_norm_rope(col(2), kns_ref[...], seg_ref, cos, sin))
    heads_out(vs_ref, 0, col(3))
    heads_out(kw_ref, 0, _head_norm_rope(col(4), knw_ref[...], seg_ref, cos, sin))
    heads_out(vw_ref, 0, col(5))
    gate_ref[...] = jax.nn.sigmoid(_dot(xn, w_ref[:, d + 6 * kd:]))


def _nsa_proj(h, g_mix, w_in, cos_t, sin_t, q_norm, k_norm):
    bsz, t, d = h.shape
    n = bsz * t
    tm = ROW_TILE
    nt = t // tm
    kd = N_KV * HEAD_DIM
    n_heads = d // HEAD_DIM
    n_in = w_in.shape[1]
    w_pad = jnp.zeros((d, d + 6 * kd + LANES), F32).at[:, :n_in].set(w_in).astype(BF16)
    tile4 = lambda v: jnp.tile(v, kd // HEAD_DIM).reshape(1, kd)
    idx = jnp.arange(kd) // HEAD_DIM
    seg = (idx[:, None] == idx[None, :]).astype(BF16)
    head_spec = lambda nh, w=HEAD_DIM: pl.BlockSpec((1, nh, tm, w), lambda i: (i // nt, 0, i % nt, 0))
    head_shape = lambda nh, w=HEAD_DIM, dt=BF16: jax.ShapeDtypeStruct((bsz, nh, t, w), dt)
    flat_spec = lambda w: pl.BlockSpec((tm, w), lambda i: (i, 0))
    vw2 = 2 * HEAD_DIM
    return pl.pallas_call(
        _nsa_proj_kernel,
        grid=(n // tm,),
        in_specs=[flat_spec(d), _full((1, d)), _full(w_pad.shape), flat_spec(LANES), flat_spec(LANES),
                  _full((1, kd)), _full((1, kd)), _full((1, kd)), _full((kd, kd))],
        out_specs=[head_spec(n_heads), head_spec(N_KV), head_spec(N_KV), head_spec(N_KV), head_spec(N_KV, vw2),
                   head_spec(N_KV), head_spec(N_KV, vw2), flat_spec(LANES)],
        out_shape=[head_shape(n_heads), head_shape(N_KV, dt=F32), head_shape(N_KV, dt=F32),
                   head_shape(N_KV), head_shape(N_KV, vw2), head_shape(N_KV), head_shape(N_KV, vw2),
                   jax.ShapeDtypeStruct((n, LANES), F32)],
        compiler_params=_params(1),
        name="nsa_proj",
    )(h.reshape(n, d), g_mix.reshape(1, d), w_pad, cos_t, sin_t,
      tile4(q_norm), tile4(k_norm[1]), tile4(k_norm[2]), seg)


def _compress_kernel(xk_ref, xv_ref, pe_ref, w1_ref, w2_ref, kn_ref, cos_ref, sin_ref, o_ref):
    nch, dh = o_ref.shape[3], o_ref.shape[4]
    st = CMP_STRIDE
    hidden = w1_ref.shape[2]
    for which, x_ref in enumerate((xk_ref, xv_ref)):
        first = jnp.zeros((nch, hidden), F32)
        second = jnp.zeros((nch, hidden), F32)
        for tp in range(st):
            xt = x_ref[0, 0, pl.ds(tp, nch, stride=st), :]
            first = first + _dot((xt + pe_ref[which, tp:tp + 1, :]).astype(BF16),
                                 w1_ref[which, tp * dh:(tp + 1) * dh, :])
            second = second + _dot((xt + pe_ref[which, st + tp:st + tp + 1, :]).astype(BF16),
                                   w1_ref[which, (st + tp) * dh:(st + tp + 1) * dh, :])
        hid = first + pltpu.roll(second, nch - 1, 0)
        c = _dot(jax.nn.gelu(hid, approximate=True).astype(BF16), w2_ref[which])
        if which == 0:
            half = dh // 2
            cn = _rms(c, kn_ref[...])
            rot = jnp.concatenate([cn[:, half:], cn[:, :half]], axis=1)
            c = cn * cos_ref[0] + rot * sin_ref[0]
        o_ref[which, 0, 0] = c.astype(o_ref.dtype)


def _compress(kc_raw, vc_raw, pe, w1, w2, k_norm0, cos_t, sin_t):
    bsz, g, t, dh = kc_raw.shape
    st = CMP_STRIDE
    nch = t // st
    last = lambda tab: jnp.pad(tab.reshape(bsz, t, LANES)[:, CMP_BLOCK - 1::st, :dh], ((0, 0), (0, 1), (0, 0)))
    src = pl.BlockSpec((1, 1, t, dh), lambda b, gi: (b, gi, 0, 0))
    tab = pl.BlockSpec((1, nch, dh), lambda b, gi: (b, 0, 0))
    return pl.pallas_call(
        _compress_kernel,
        grid=(bsz, g),
        in_specs=[src, src, _full(pe.shape), _full(w1.shape), _full(w2.shape), _full((1, dh)), tab, tab],
        out_specs=pl.BlockSpec((2, 1, 1, nch, dh), lambda b, gi: (0, b, gi, 0, 0)),
        out_shape=jax.ShapeDtypeStruct((2, bsz, g, nch, dh), BF16),
        compiler_params=_params(2),
        name="nsa_compress",
    )(kc_raw, vc_raw, pe, w1.astype(BF16), w2.astype(BF16), k_norm0.reshape(1, dh), last(cos_t), last(sin_t))


def _attend_chunk(q, k, v, bias, m_ref, acc_ref, r_heads, tq_n):
    tk = k.shape[0]
    s = _dot_nt(q, k)
    m_all = m_ref[...]
    acc_all = acc_ref[...]
    m_out, scaled, probs = [], [], []
    for r in range(r_heads):
        rs = slice(r * tq_n, (r + 1) * tq_n)
        sr = s[rs] + bias
        m_old = m_all[rs]
        m_new = jnp.maximum(m_old, jnp.max(sr, axis=-1, keepdims=True))
        m_wide = m_new if tk == m_new.shape[1] else jnp.concatenate([m_new] * (tk // m_new.shape[1]), axis=1)
        probs.append(jnp.exp2(sr - m_wide).astype(BF16))
        m_out.append(m_new)
        scaled.append(jnp.exp2(m_old - m_new) * acc_all[rs])
    m_ref[...] = jnp.concatenate(m_out, axis=0)
    acc_ref[...] = jnp.concatenate(scaled, axis=0) + _dot(jnp.concatenate(probs, axis=0), v)


def _nsa_attn_kernel(q_ref, kc_ref, vc_ref, ks_ref, vs_ref, kw_ref, vw_ref, gate_ref, wsel_ref, e_ref, o_ref,
                     m_ref, acc_ref, og_ref, bias_ref, *, n_sel):
    g = pl.program_id(1)
    qi = pl.program_id(2)
    r_heads, tq_n, dh = q_ref.shape[1], q_ref.shape[2], q_ref.shape[3]
    rows = r_heads * tq_n
    t0 = qi * tq_n
    tk = K_TILE
    q = q_ref[0].reshape(rows, dh)
    head_rows = [slice(r * tq_n, (r + 1) * tq_n) for r in range(r_heads)]

    gates = gate_ref[0]
    gate_lane = lax.broadcasted_iota(I32, gates.shape, 1)

    def gate(r, branch):
        col = (g * r_heads + r) * 3 + branch
        return jnp.sum(jnp.where(gate_lane == col, gates, 0.0), axis=-1, keepdims=True)

    def restart():
        m_ref[...] = jnp.full(m_ref.shape, NEG, F32)
        acc_ref[...] = jnp.zeros(acc_ref.shape, F32)

    def add_branch(branch):
        acc = acc_ref[...]
        o = acc * (1.0 / pltpu.roll(acc, dh, 1))
        og_ref[...] += jnp.concatenate([gate(r, branch) * o[rs, :dh] for r, rs in enumerate(head_rows)], axis=0)

    kc = kc_ref[0, 0]
    nc = kc.shape[0]
    tq_c = t0 + lax.broadcasted_iota(I32, (tq_n, nc), 0)
    mask_c = CMP_STRIDE * lax.broadcasted_iota(I32, (tq_n, nc), 1) + (CMP_BLOCK - 1) <= tq_c
    s_c = _dot_nt(q, kc)
    imp = jnp.zeros((tq_n, nc), F32)
    probs = []
    for rs in head_rows:
        sr = jnp.where(mask_c, s_c[rs], NEG)
        e = jnp.where(mask_c, jnp.exp2(sr - jnp.max(sr, axis=-1, keepdims=True)), 0.0)
        l = jnp.sum(e, axis=-1, keepdims=True)
        p = e * jnp.where(l > 0.0, 1.0 / l, 0.0)
        imp = imp + p
        probs.append(p.astype(BF16))
    o_c = _dot(jnp.concatenate(probs, axis=0), vc_ref[0, 0])
    og_c = jnp.concatenate([gate(r, 0) * o_c[rs] for r, rs in enumerate(head_rows)], axis=0)

    imp_hi = imp.astype(BF16)
    imp_lo = (imp - imp_hi.astype(F32)).astype(BF16)
    p_slc = _dot_nt(wsel_ref[...], imp_hi) + _dot_nt(wsel_ref[...], imp_lo)
    nsb = p_slc.shape[0]
    blk = lax.broadcasted_iota(I32, (nsb, tq_n), 0)
    tq_t = t0 + lax.broadcasted_iota(I32, (nsb, tq_n), 1)
    cur = tq_t // SLC_BLOCK
    forced = (blk == 0) | (blk == cur) | (blk == cur - 1)
    score = jnp.where(forced, jnp.inf, jnp.where(blk * SLC_BLOCK <= tq_t, p_slc, -jnp.inf))
    sub = CONV_SUBLANES
    groups = [slice(g0, g0 + sub) for g0 in range(0, nsb, sub)]
    cnt = [jnp.zeros((sub, tq_n), F32) for _ in groups]
    for i in range(nsb):
        ri = score[i:i + 1, :]
        for gi, rows_g in enumerate(groups):
            sg = score[rows_g]
            if rows_g.start > i:
                beats = ri >= sg
            elif rows_g.stop <= i:
                beats = ri > sg
            else:
                idx = rows_g.start + lax.broadcasted_iota(I32, (sub, tq_n), 0)
                beats = (ri > sg) | ((ri == sg) & (idx > i))
            cnt[gi] = cnt[gi] + jnp.where(beats, 1.0, 0.0)
    dropped = jnp.where(jnp.concatenate(cnt, axis=0) < n_sel, 0.0, 1.0)
    dropped = jnp.concatenate([dropped, jnp.zeros((LANES - nsb, tq_n), F32)], axis=0).T.astype(BF16)

    kwn = WINDOW + tq_n
    start = pl.multiple_of(jnp.maximum(t0 - WINDOW, 0), tq_n)
    kpos_w = start + lax.broadcasted_iota(I32, (tq_n, kwn), 1)
    tq_w = t0 + lax.broadcasted_iota(I32, (tq_n, kwn), 0)
    bias_w = jnp.where((kpos_w <= tq_w) & (kpos_w > tq_w - WINDOW), 0.0, NEG)
    s_w = _dot_nt(q, kw_ref[0, 0, pl.ds(start, kwn), :])
    probs = []
    for rs in head_rows:
        sr = s_w[rs] + bias_w
        probs.append(jnp.exp2(sr - jnp.max(sr, axis=-1, keepdims=True)).astype(BF16))
    acc_w = _dot(jnp.concatenate(probs, axis=0), vw_ref[0, 0, pl.ds(start, kwn), :])
    o_w = acc_w * (1.0 / pltpu.roll(acc_w, dh, 1))
    og_ref[...] = og_c + jnp.concatenate([gate(r, 2) * o_w[rs, :dh] for r, rs in enumerate(head_rows)], axis=0)

    t_keys = e_ref.shape[1]
    kpos = lax.broadcasted_iota(I32, (tq_n, t_keys), 1)
    tq_s = t0 + lax.broadcasted_iota(I32, (tq_n, t_keys), 0)
    bias_all = jnp.where(kpos <= tq_s, _dot(dropped, e_ref[...]), NEG)
    for c in range(t_keys // tk):
        bias_ref[c] = bias_all[:, c * tk:(c + 1) * tk]
    restart()

    def sel_step(c, carry):
        k0 = pl.multiple_of(c * tk, tk)
        _attend_chunk(q, ks_ref[0, 0, pl.ds(k0, tk), :], vs_ref[0, 0, pl.ds(k0, tk), :], bias_ref[c],
                      m_ref, acc_ref, r_heads, tq_n)
        return carry

    lax.fori_loop(0, (t0 + tq_n + tk - 1) // tk, sel_step, 0)
    add_branch(1)

    for r, rs in enumerate(head_rows):
        o_ref[0, :, r * dh:(r + 1) * dh] = og_ref[rs].astype(o_ref.dtype)


def _nsa_attn(q, kc, vc, ks, vs, kw, vw, gate, n_sel):
    bsz, n_heads, t, dh = q.shape
    g = N_KV
    r_heads = n_heads // g
    tq = Q_TILE
    nch = kc.shape[2]
    a = SLC_BLOCK // CMP_STRIDE
    bb = CMP_BLOCK // CMP_STRIDE
    nsb = t // SLC_BLOCK
    j = jnp.arange(nsb)[:, None]
    c = jnp.arange(nch)[None, :]
    wsel = sum(((c == a * j + m + n_) & (c < nch - 1)).astype(F32)
               for m in range(a) for n_ in range(bb)).astype(BF16)
    tk = K_TILE
    key_blk = (jnp.arange(t) // SLC_BLOCK).reshape(1, t)
    e = jnp.where(key_blk == jnp.arange(LANES).reshape(LANES, 1), NEG, 0.0).astype(BF16)
    rows = r_heads * tq
    kv_spec = lambda n_rows, w=dh: pl.BlockSpec((1, 1, n_rows, w), lambda b, gi, qi: (b, gi, 0, 0))
    return pl.pallas_call(
        functools.partial(_nsa_attn_kernel, n_sel=n_sel),
        grid=(bsz, g, t // tq),
        in_specs=[
            pl.BlockSpec((1, r_heads, tq, dh), lambda b, gi, qi: (b, gi, qi, 0)),
            kv_spec(nch), kv_spec(nch), kv_spec(t), kv_spec(t, 2 * dh), kv_spec(t), kv_spec(t, 2 * dh),
            pl.BlockSpec((1, tq, LANES), lambda b, gi, qi: (b, qi, 0)),
            pl.BlockSpec((nsb, nch), lambda b, gi, qi: (0, 0)),
            pl.BlockSpec(e.shape, lambda b, gi, qi: (0, 0)),
        ],
        out_specs=pl.BlockSpec((1, tq, r_heads * dh), lambda b, gi, qi: (b, qi, gi)),
        out_shape=jax.ShapeDtypeStruct((bsz, t, n_heads * dh), BF16),
        scratch_shapes=[pltpu.VMEM((rows, 2 * dh), F32), pltpu.VMEM((rows, 2 * dh), F32),
                        pltpu.VMEM((rows, dh), F32), pltpu.VMEM((t // tk, tq, tk), F32)],
        compiler_params=_params(3),
        name="nsa_attention",
    )(q, kc, vc, ks, vs, kw, vw, gate, wsel, e)


def _out_mix_kernel(o_ref, w_ref, h_ref, gffn_ref, wcat_ref, br_ref, tri_ref,
                    h2_ref, xn_ref, slab_ref, slabt_ref, cnt_ref, carry_ref):
    @pl.when(pl.program_id(0) == 0)
    def _():
        carry_ref[...] = jnp.zeros_like(carry_ref)

    h = h_ref[...] + _dot(o_ref[...], w_ref[...])
    h2_ref[...] = h
    xn, slab = _route(h, gffn_ref[...], wcat_ref, br_ref, tri_ref, carry_ref)
    xn_ref[...] = xn
    slab_ref[...] = slab
    slabt_ref[...] = slab.T[:ROUTE_FIELDS]
    cnt_ref[...] = carry_ref[...]


def _out_mix(o2, w_out, h2, g_ffn, w_rg, b_rg, w_re, b_re):
    n, d = h2.shape
    tm = ROW_TILE
    wcat, br, tri = _router_operands(w_rg, b_rg, w_re, b_re, tm)
    row = pl.BlockSpec((tm, d), lambda i: (i, 0))
    return pl.pallas_call(
        _out_mix_kernel,
        grid=(n // tm,),
        in_specs=[row, _full((d, d)), row, _full((1, d)), _full((d, 2 * LANES)), _full((1, LANES)), _full((tm, tm))],
        out_specs=[row, row, pl.BlockSpec((tm, LANES), lambda i: (i, 0)),
                   pl.BlockSpec((ROUTE_FIELDS, tm), lambda i: (0, i)), _full((1, LANES))],
        out_shape=[jax.ShapeDtypeStruct((n, d), F32), jax.ShapeDtypeStruct((n, d), F32),
                   jax.ShapeDtypeStruct((n, LANES), F32), jax.ShapeDtypeStruct((ROUTE_FIELDS, n), F32),
                   jax.ShapeDtypeStruct((1, LANES), F32)],
        scratch_shapes=[pltpu.VMEM((1, LANES), F32)],
        compiler_params=_params(1),
        name="nsa_out_mix",
    )(o2, w_out.astype(BF16), h2, g_ffn.reshape(1, d), wcat, br, tri)


def _nsa_layer(h, positions, g_mix, w_in, q_norm, k_norm, cmp_pe, cmp_w1, cmp_w2):
    bsz, t, d = h.shape
    cos_t, sin_t = _rope_table(positions)
    q, kc_raw, vc_raw, ks, vs, kw, vw, gate = _nsa_proj(h, g_mix, w_in, cos_t, sin_t, q_norm, k_norm)
    cmp = _compress(kc_raw, vc_raw, cmp_pe, cmp_w1, cmp_w2, k_norm[0], cos_t, sin_t)
    n_sel = min(N_SEL, t // SLC_BLOCK)
    return _nsa_attn(q, cmp[0], cmp[1], ks, vs, kw, vw, gate.reshape(bsz, t, LANES), n_sel)


def kernel(x, p, positions, g_mix, g_ffn, g_ple, conv_w_pw1, conv_b_pw1, conv_w_dw, conv_b_dw, conv_ln_g, conv_ln_b, conv_w_pw2, conv_b_pw2, nsa_w_in, nsa_q_norm, nsa_k_norm, nsa_cmp_pe, nsa_cmp_w1, nsa_cmp_w2, nsa_w_out, moe_w_rg, moe_b_rg, moe_w_re, moe_b_re, moe_w_gate, moe_w_up, moe_w_down, ple_w_proj, ple_w_gate):
    bsz, t, d = x.shape
    n = bsz * t
    depth = p.shape[0]
    h = x
    p3 = p.reshape(depth, n, p.shape[-1])
    for i in range(depth):
        j = i // 2
        route_w = (g_ffn[i], moe_w_rg[i], moe_b_rg[i], moe_w_re[i], moe_b_re[i])
        if i % 2 == 0:
            u = _pw1_glu(h.reshape(n, d), g_mix[i], conv_w_pw1[j], conv_b_pw1[j])
            h, xn, slab, slab_t, counts = _conv_mix(u.reshape(bsz, t, d), h, conv_w_dw[j], conv_b_dw[j],
                                                    conv_ln_g[j], conv_ln_b[j], conv_w_pw2[j], conv_b_pw2[j], *route_w)
        else:
            o = _nsa_layer(h, positions, g_mix[i], nsa_w_in[j], nsa_q_norm[j], nsa_k_norm[j],
                           nsa_cmp_pe[j], nsa_cmp_w1[j], nsa_cmp_w2[j])
            h, xn, slab, slab_t, counts = _out_mix(o.reshape(n, d), nsa_w_out[j], h.reshape(n, d), *route_w)
        h = _moe_ple(h.reshape(n, d), xn, slab, slab_t, counts, i, p3, g_ple[i],
                     moe_w_gate, moe_w_up, moe_w_down, ple_w_gate[i], ple_w_proj[i]).reshape(bsz, t, d)
    return h
```

```python
import functools

import jax
import jax.numpy as jnp
from jax import lax
from jax.experimental import pallas as pl
from jax.experimental.pallas import tpu as pltpu

F32 = jnp.float32
BF16 = jnp.bfloat16
I32 = jnp.int32

LANES = 128
VMEM_LIMIT = 56 * 1024 * 1024

EPS = 1e-6
NEG = -1e30
LOG2_E = 1.4426950408889634

CONV_WIDTH = 31
HEAD_DIM = 64
N_KV = 4
CMP_BLOCK = 32
CMP_STRIDE = 16
SLC_BLOCK = 64
N_SEL = 16
WINDOW = 512
ROPE_THETA = 10000.0
N_GROUPS = 4
EXPERTS_PER_GROUP = 8
N_EXPERTS = N_GROUPS * EXPERTS_PER_GROUP

ROW_TILE = 512
CONV_CHUNK = 64
CONV_ROWS = 128
CONV_SUBLANES = 8
CONV_HALO = 32
EXPERT_ROWS = 256
Q_TILE = 512
K_TILE = 512
WINDOW_Q_TILE = 128
ROUTE_FIELDS = 8


def _params(n_grid):
    return pltpu.CompilerParams(dimension_semantics=("arbitrary",) * n_grid,
                                vmem_limit_bytes=VMEM_LIMIT)


def _dot(a, b):
    return jnp.dot(a, b, preferred_element_type=F32)


def _dot_nt(a, b):
    return lax.dot_general(a, b, (((1,), (1,)), ((), ())), preferred_element_type=F32)


def _rms(x, g):
    return x * lax.rsqrt(jnp.mean(x * x, axis=-1, keepdims=True) + EPS) * g


def _full(shape):
    n = len(shape)
    return pl.BlockSpec(shape, lambda *_: (0,) * n)


def _route(h, g_ffn, wcat_ref, br_ref, tri_ref, carry_ref):
    xn = _rms(h, g_ffn)
    hi = xn.astype(BF16)
    lo = (xn - hi.astype(F32)).astype(BF16)
    r1 = _dot(hi, wcat_ref[...])
    r2 = _dot(lo, wcat_ref[:, :LANES])
    logits = r1[:, :LANES] + r1[:, LANES:] + r2 + br_ref[...]

    lane = lax.broadcasted_iota(I32, logits.shape, 1).astype(F32)
    lg = jnp.where(lane < N_GROUPS, logits, -jnp.inf)
    m = jnp.max(lg, axis=-1, keepdims=True)
    grp = jnp.min(jnp.where(lg == m, lane, float(LANES)), axis=-1, keepdims=True)
    pg_sel = 1.0 / jnp.sum(jnp.exp(lg - m), axis=-1, keepdims=True)

    lo_e = N_GROUPS + EXPERTS_PER_GROUP * grp
    le = jnp.where((lane >= lo_e) & (lane < lo_e + EXPERTS_PER_GROUP), logits, -jnp.inf)
    v1 = jnp.max(le, axis=-1, keepdims=True)
    i1 = jnp.min(jnp.where(le == v1, lane, float(LANES)), axis=-1, keepdims=True)
    le2 = jnp.where(lane == i1, -jnp.inf, le)
    v2 = jnp.max(le2, axis=-1, keepdims=True)
    i2 = jnp.min(jnp.where(le2 == v2, lane, float(LANES)), axis=-1, keepdims=True)
    e21 = jnp.exp(v2 - v1)
    w1 = pg_sel / (1.0 + e21)
    w2 = pg_sel * e21 / (1.0 + e21)
    e1 = i1 - N_GROUPS
    e2 = i2 - N_GROUPS

    oh1 = jnp.where(lane == e1, 1.0, 0.0)
    oh2 = jnp.where(lane == e2, 1.0, 0.0)
    c1 = _dot(tri_ref[...], oh1.astype(BF16))
    c2 = _dot(tri_ref[...], oh2.astype(BF16))
    carry = carry_ref[...]
    tot1 = jnp.sum(oh1, axis=0, keepdims=True)
    tot2 = jnp.sum(oh2, axis=0, keepdims=True)
    rank1 = jnp.sum(oh1 * (c1 + carry), axis=-1, keepdims=True)
    rank2 = jnp.sum(oh2 * (c2 + carry + tot1), axis=-1, keepdims=True)
    carry_ref[...] = carry + tot1 + tot2

    slab = jnp.where(lane == 0, e1, jnp.where(lane == 1, e2, jnp.where(lane == 2, w1, jnp.where(
        lane == 3, w2, jnp.where(lane == 4, rank1, jnp.where(lane == 5, rank2, 0.0))))))
    return xn, slab


def _router_operands(w_rg, b_rg, w_re, b_re, tm):
    d = w_rg.shape[0]
    w = jnp.zeros((d, LANES), F32).at[:, :N_GROUPS].set(w_rg).at[:, N_GROUPS:N_GROUPS + N_EXPERTS].set(w_re)
    w_hi = w.astype(BF16)
    w_lo = (w - w_hi.astype(F32)).astype(BF16)
    wcat = jnp.concatenate([w_hi, w_lo], axis=1)
    br = jnp.zeros((1, LANES), F32).at[0, :N_GROUPS].set(b_rg).at[0, N_GROUPS:N_GROUPS + N_EXPERTS].set(b_re)
    r = lax.broadcasted_iota(I32, (tm, tm), 0)
    c = lax.broadcasted_iota(I32, (tm, tm), 1)
    tri = (c < r).astype(BF16)
    return wcat, br, tri


def _pw1_glu_kernel(x_ref, g_ref, w_ref, b_ref, o_ref, *, chunk):
    xn = _rms(x_ref[...], g_ref[...]).astype(BF16)
    d = o_ref.shape[1]
    for j in range(d // chunk):
        sa = slice(j * chunk, (j + 1) * chunk)
        sg = slice(d + j * chunk, d + (j + 1) * chunk)
        a = _dot(xn, w_ref[:, sa]) + b_ref[:, sa]
        g = _dot(xn, w_ref[:, sg]) + b_ref[:, sg]
        o_ref[:, sa] = a * jax.nn.sigmoid(g)


def _pw1_glu(x2, g, w, b):
    n, d = x2.shape
    tm = ROW_TILE
    return pl.pallas_call(
        functools.partial(_pw1_glu_kernel, chunk=512),
        grid=(n // tm,),
        in_specs=[pl.BlockSpec((tm, d), lambda i: (i, 0)), _full((1, d)), _full((d, 2 * d)), _full((1, 2 * d))],
        out_specs=pl.BlockSpec((tm, d), lambda i: (i, 0)),
        out_shape=jax.ShapeDtypeStruct((n, d), F32),
        compiler_params=_params(1),
        name="pw1_glu",
    )(x2, g.reshape(1, d), w.astype(BF16), b.reshape(1, 2 * d))


def _conv_mix_kernel(u_ref, halo_ref, x_ref, wdw_ref, bdw_ref, lng_ref, lnb_ref, w2_ref, b2_ref,
                     gffn_ref, wcat_ref, br_ref, tri_ref,
                     h_ref, xn_ref, slab_ref, slabt_ref, cnt_ref,
                     ext_ref, conv_ref, act_ref, carry_ref):
    b = pl.program_id(0)
    i = pl.program_id(1)
    tt = u_ref.shape[1]

    @pl.when((b == 0) & (i == 0))
    def _():
        carry_ref[...] = jnp.zeros_like(carry_ref)

    halo = halo_ref[0]
    ext_ref[0:CONV_HALO, :] = jnp.where(i == 0, jnp.zeros_like(halo), halo)
    ext_ref[CONV_HALO:CONV_HALO + tt, :] = u_ref[0]
    ext_ref[CONV_HALO + tt:, :] = jnp.zeros((CONV_SUBLANES, ext_ref.shape[1]), F32)
    lead = CONV_HALO - (CONV_WIDTH - 1)

    sub = CONV_SUBLANES
    for lt in range(ext_ref.shape[1] // LANES):
        cols = slice(lt * LANES, (lt + 1) * LANES)

        def conv_block(c, carry, cols=cols):
            base = pl.multiple_of(c * CONV_ROWS, CONV_ROWS)
            acc = jnp.zeros((CONV_ROWS, LANES), F32)
            for rem in range(sub):
                part = None
                for a in range(CONV_HALO // sub + 1):
                    k = rem + sub * a - lead
                    if 0 <= k < CONV_WIDTH:
                        term = wdw_ref[k:k + 1, cols] * ext_ref[pl.ds(base + sub * a, CONV_ROWS + sub), cols]
                        part = term if part is None else part + term
                acc = acc + part[rem:rem + CONV_ROWS]
            conv_ref[pl.ds(base, CONV_ROWS), cols] = acc
            return carry

        lax.fori_loop(0, tt // CONV_ROWS, conv_block, 0)

    def chunk(c, carry):
        base = pl.multiple_of(c * CONV_CHUNK, CONV_CHUNK)
        acc = conv_ref[pl.ds(base, CONV_CHUNK), :] + bdw_ref[...]
        mu = jnp.mean(acc, axis=-1, keepdims=True)
        cen = acc - mu
        var = jnp.mean(cen * cen, axis=-1, keepdims=True)
        y = cen * lax.rsqrt(var + EPS) * lng_ref[...] + lnb_ref[...]
        act_ref[pl.ds(base, CONV_CHUNK), :] = (y * jax.nn.sigmoid(y)).astype(BF16)
        return carry

    lax.fori_loop(0, tt // CONV_CHUNK, chunk, 0)

    h = x_ref[0] + _dot(act_ref[...], w2_ref[...]) + b2_ref[...]
    h_ref[0] = h
    xn, slab = _route(h, gffn_ref[...], wcat_ref, br_ref, tri_ref, carry_ref)
    xn_ref[...] = xn
    slab_ref[...] = slab
    slabt_ref[...] = slab.T[:ROUTE_FIELDS]
    cnt_ref[...] = carry_ref[...]


def _conv_mix(u, x, w_dw, b_dw, ln_g, ln_b, w_pw2, b_pw2, g_ffn, w_rg, b_rg, w_re, b_re):
    bsz, t, d = x.shape
    tt = ROW_TILE
    nt = t // tt
    n = bsz * t
    wcat, br, tri = _router_operands(w_rg, b_rg, w_re, b_re, tt)
    wdw = jnp.zeros((CONV_HALO, d), F32).at[:CONV_WIDTH].set(w_dw)
    hb = tt // CONV_HALO
    row = lambda v: v.reshape(1, d)
    return pl.pallas_call(
        _conv_mix_kernel,
        grid=(bsz, nt),
        in_specs=[
            pl.BlockSpec((1, tt, d), lambda b, i: (b, i, 0)),
            pl.BlockSpec((1, CONV_HALO, d), lambda b, i: (b, jnp.maximum(i * hb - 1, 0), 0)),
            pl.BlockSpec((1, tt, d), lambda b, i: (b, i, 0)),
            _full((CONV_HALO, d)), _full((1, d)), _full((1, d)), _full((1, d)),
            _full((d, d)), _full((1, d)), _full((1, d)),
            _full((d, 2 * LANES)), _full((1, LANES)), _full((tt, tt)),
        ],
        out_specs=[
            pl.BlockSpec((1, tt, d), lambda b, i: (b, i, 0)),
            pl.BlockSpec((tt, d), lambda b, i: (b * nt + i, 0)),
            pl.BlockSpec((tt, LANES), lambda b, i: (b * nt + i, 0)),
            pl.BlockSpec((ROUTE_FIELDS, tt), lambda b, i: (0, b * nt + i)),
            _full((1, LANES)),
        ],
        out_shape=[
            jax.ShapeDtypeStruct((bsz, t, d), F32),
            jax.ShapeDtypeStruct((n, d), F32),
            jax.ShapeDtypeStruct((n, LANES), F32),
            jax.ShapeDtypeStruct((ROUTE_FIELDS, n), F32),
            jax.ShapeDtypeStruct((1, LANES), F32),
        ],
        scratch_shapes=[pltpu.VMEM((CONV_HALO + tt + CONV_SUBLANES, d), F32), pltpu.VMEM((tt, d), F32),
                        pltpu.VMEM((tt, d), BF16),
                        pltpu.VMEM((1, LANES), F32)],
        compiler_params=_params(2),
        name="conv_mix",
    )(u, u, x, wdw, row(b_dw), row(ln_g), row(ln_b), w_pw2.astype(BF16), row(b_pw2), row(g_ffn), wcat, br, tri)


def _invert_slot_map(dest_ref, meta_ref, asg_ref):
    n_asg = dest_ref.shape[0]
    groups = meta_ref.shape[0] // 3

    def pad_group(e, carry):
        off = meta_ref[2 * groups + e]

        def body(s, c):
            asg_ref[s] = s + off
            return c

        lax.fori_loop(meta_ref[e], meta_ref[groups + e], body, 0)
        return carry

    lax.fori_loop(0, groups, pad_group, 0)

    def real(a, carry):
        asg_ref[dest_ref[a]] = a
        return carry

    lax.fori_loop(0, n_asg, real, 0, unroll=8)


def _expert_fused_kernel(be_ref, nu_ref, dest_ref, meta_ref, xn_hbm, wg_ref, wu_ref, wd_ref, y_hbm,
                         asg_ref, xbuf, ybuf, wgb_ref, wub_ref, wdb_ref, gsem, ssem, *, n_tok):
    i = pl.program_id(0)
    nu = nu_ref[0]
    be = xbuf.shape[1]
    slot = i % 2
    other = 1 - slot

    @pl.when(i == 0)
    def _():
        _invert_slot_map(dest_ref, meta_ref, asg_ref)

    def gather_copy(blk, buf, r):
        a = asg_ref[blk * be + r]
        tok = a & (n_tok - 1) if n_tok & (n_tok - 1) == 0 else lax.rem(a, n_tok)
        return pltpu.make_async_copy(xn_hbm.at[pl.ds(tok, 1)], xbuf.at[buf, pl.ds(r, 1)], gsem.at[buf])

    def scatter_copy(blk, buf, r):
        return pltpu.make_async_copy(ybuf.at[buf, pl.ds(r, 1)], y_hbm.at[pl.ds(asg_ref[blk * be + r], 1)],
                                     ssem.at[buf])

    def for_rows(fn):
        def body(r, carry):
            fn(r)
            return carry

        lax.fori_loop(0, be, body, 0, unroll=8)

    @pl.when(i == 0)
    def _():
        for_rows(lambda r: gather_copy(0, 0, r).start())
        ybuf[1] = jnp.zeros(ybuf.shape[1:], ybuf.dtype)

    def expert_step(cur, oth):
        for_rows(lambda r: gather_copy(0, cur, r).wait())

        @pl.when((i == 0) | (be_ref[i] != be_ref[jnp.maximum(i - 1, 0)]))
        def _():
            wgb_ref[...] = wg_ref[0, 0].astype(BF16)
            wub_ref[...] = wu_ref[0, 0].astype(BF16)
            wdb_ref[...] = wd_ref[0, 0].astype(BF16)

        nxt = jnp.minimum(i + 1, nu - 1)
        prv = jnp.maximum(i - 1, 0)
        for r in range(be):
            gather_copy(nxt, oth, r).start(priority=r % 2)
            scatter_copy(prv, oth, r).start(priority=(r + 1) % 2)
        x = xbuf[cur].astype(BF16)
        a = _dot(x, wgb_ref[...])
        u = _dot(x, wub_ref[...])
        hid = (a * jax.nn.sigmoid(a) * u).astype(BF16)
        ybuf[cur] = _dot(hid, wdb_ref[...])
        for_rows(lambda r: scatter_copy(0, oth, r).wait())

    for half in (0, 1):
        pl.when((i < nu) & (slot == half))(functools.partial(expert_step, half, 1 - half))

    @pl.when(i == nu)
    def _():
        for_rows(lambda r: gather_copy(0, slot, r).wait())
        for_rows(lambda r: scatter_copy(nu - 1, other, r).start())
        for_rows(lambda r: scatter_copy(0, other, r).wait())
        xbuf[slot] = jnp.zeros(xbuf.shape[1:], xbuf.dtype)

        def zero_copy(blk):
            return pltpu.make_async_copy(xbuf.at[slot], y_hbm.at[pl.ds(pl.multiple_of(blk * be, be), be)],
                                         gsem.at[slot])

        def start(blk, carry):
            zero_copy(blk).start()
            return carry

        def finish(blk, carry):
            zero_copy(blk).wait()
            return carry

        lax.fori_loop(nu, pl.num_programs(0), start, 0)
        lax.fori_loop(nu, pl.num_programs(0), finish, 0)


def _experts_fused(blk_e, n_used, dest_flat, meta, n_slots, xn, layer, w_gate, w_up, w_down):
    n, d = xn.shape
    f = w_gate.shape[3]
    be = EXPERT_ROWS
    wspec = lambda a, b: pl.BlockSpec((1, 1, a, b), lambda i, e, nu, dest, meta: (layer, e[i], 0, 0))
    return pl.pallas_call(
        functools.partial(_expert_fused_kernel, n_tok=n),
        grid_spec=pltpu.PrefetchScalarGridSpec(
            num_scalar_prefetch=4,
            grid=(n_slots // be,),
            in_specs=[pl.BlockSpec(memory_space=pl.ANY), wspec(d, f), wspec(d, f), wspec(f, d)],
            out_specs=pl.BlockSpec(memory_space=pl.ANY),
            scratch_shapes=[pltpu.SMEM((n_slots,), I32),
                            pltpu.VMEM((2, be, d), F32), pltpu.VMEM((2, be, d), F32),
                            pltpu.VMEM((d, f), BF16), pltpu.VMEM((d, f), BF16), pltpu.VMEM((f, d), BF16),
                            pltpu.SemaphoreType.DMA((2,)), pltpu.SemaphoreType.DMA((2,))],
        ),
        out_shape=jax.ShapeDtypeStruct((n_slots, d), F32),
        compiler_params=_params(1),
        name="moe_experts",
    )(blk_e, n_used, dest_flat, meta, xn, w_gate, w_up, w_down)


def _ple_kernel(h_ref, y0_ref, y1_ref, slab_ref, p_ref, gple_ref, wg_ref, wp_ref, o_ref):
    slab = slab_ref[...]
    lane = lax.broadcasted_iota(I32, slab.shape, 1)
    w0 = jnp.sum(jnp.where(lane == 2, slab, 0.0), axis=-1, keepdims=True)
    w1 = jnp.sum(jnp.where(lane == 3, slab, 0.0), axis=-1, keepdims=True)
    hm = h_ref[...] + w0 * y0_ref[...] + w1 * y1_ref[...]
    gate = jax.nn.sigmoid(_dot(_rms(hm, gple_ref[...]).astype(BF16), wg_ref[...]))
    o_ref[...] = hm + gate * _dot(p_ref[0].astype(BF16), wp_ref[...])


def _ple(h2, y, slab, layer, p3, g_ple, w_gate, w_proj):
    n, d = h2.shape
    dp = p3.shape[2]
    tm = ROW_TILE
    nt = n // tm
    row = lambda w: pl.BlockSpec((tm, w), lambda i: (i, 0))
    return pl.pallas_call(
        _ple_kernel,
        grid=(nt,),
        in_specs=[row(d), row(d), pl.BlockSpec((tm, d), lambda i: (nt + i, 0)), row(LANES),
                  pl.BlockSpec((1, tm, dp), lambda i: (layer, i, 0)), _full((1, d)), _full((d, d)), _full((dp, d))],
        out_specs=row(d),
        out_shape=jax.ShapeDtypeStruct((n, d), F32),
        compiler_params=_params(1),
        name="moe_combine_ple",
    )(h2, y, y, slab, p3, g_ple.reshape(1, d), w_gate.astype(BF16), w_proj.astype(BF16))


def _moe_ple(h2, xn, slab, slab_t, counts, layer, p3, g_ple, w_gate_e, w_up_e, w_down_e, ple_w_gate, ple_w_proj):
    n, d = h2.shape
    be = EXPERT_ROWS
    n_blocks = (2 * n) // be + N_EXPERTS
    cnt = counts[0, :N_EXPERTS].astype(I32)
    padded = (cnt + be - 1) // be * be
    ends = jnp.cumsum(padded)
    starts = ends - padded
    eid = slab_t[0:2].astype(I32)
    rank = slab_t[4:6].astype(I32)
    expert_ids = jnp.arange(N_EXPERTS, dtype=I32).reshape(N_EXPERTS, 1, 1)
    slot_base = jnp.sum(jnp.where(eid[None] == expert_ids, starts.reshape(N_EXPERTS, 1, 1), 0), axis=0)
    dest_flat = (slot_base + rank).reshape(2 * n)
    block_start = jnp.arange(n_blocks, dtype=I32) * be
    blk_e = jnp.minimum(jnp.sum((ends[None, :] <= block_start[:, None]).astype(I32), axis=1), N_EXPERTS - 1)
    n_used = (ends[-1:] // be).astype(I32)
    seen = jnp.cumsum(cnt)
    pad_lo = jnp.concatenate([starts + cnt, ends[-1:]])
    pad_hi = jnp.concatenate([ends, jnp.full((1,), n_blocks * be, I32)])
    pad_off = jnp.concatenate([2 * n - seen, jnp.zeros((1,), I32)])
    meta = jnp.concatenate([pad_lo, pad_hi, pad_off]).astype(I32)
    y = _experts_fused(blk_e, n_used, dest_flat, meta, n_blocks * be, xn, layer, w_gate_e, w_up_e, w_down_e)
    return _ple(h2, y, slab, layer, p3, g_ple, ple_w_gate, ple_w_proj)


def _rope_table_kernel(pos_ref, inv_ref, sign_ref, cos_ref, sin_ref):
    ang = pos_ref[...] * inv_ref[...]
    cos_ref[...] = jnp.cos(ang)
    sin_ref[...] = jnp.sin(ang) * sign_ref[...]


def _rope_table(positions):
    n = positions.size
    tm = ROW_TILE
    half = HEAD_DIM // 2
    lane = jnp.arange(LANES)
    inv = 1.0 / (ROPE_THETA ** ((lane % half).astype(F32) / half))
    sign = jnp.where(lane % HEAD_DIM < half, -1.0, 1.0).astype(F32)
    return pl.pallas_call(
        _rope_table_kernel,
        grid=(n // tm,),
        in_specs=[pl.BlockSpec((tm, 1), lambda i: (i, 0)), _full((1, LANES)), _full((1, LANES))],
        out_specs=[pl.BlockSpec((tm, LANES), lambda i: (i, 0))] * 2,
        out_shape=[jax.ShapeDtypeStruct((n, LANES), F32)] * 2,
        compiler_params=_params(1),
        name="rope_table",
    )(positions.astype(F32).reshape(n, 1), inv.reshape(1, LANES), sign.reshape(1, LANES))


def _head_norm_rope(z, gain, seg_ref, cos, sin):
    z2 = z * z
    hi = z2.astype(BF16)
    lo = (z2 - hi.astype(F32)).astype(BF16)
    ssq = _dot(hi, seg_ref[...]) + _dot(lo, seg_ref[...])
    zn = z * lax.rsqrt(ssq * (1.0 / HEAD_DIM) + EPS) * gain
    width = z.shape[1]
    half = HEAD_DIM // 2
    lane = lax.broadcasted_iota(I32, z.shape, 1)
    rot = jnp.where((lane & half) == 0, pltpu.roll(zn, width - half, 1), pltpu.roll(zn, half, 1))
    return zn * cos + rot * sin


def _nsa_proj_kernel(h_ref, g_ref, w_ref, cos_ref, sin_ref, qn_ref, kns_ref, knw_ref, seg_ref,
                     q_ref, kc_ref, vc_ref, ks_ref, vs_ref, kw_ref, vw_ref, gate_ref):
    d = h_ref.shape[1]
    kd = N_KV * HEAD_DIM
    xn = _rms(h_ref[...], g_ref[...]).astype(BF16)
    cos = jnp.concatenate([cos_ref[...], cos_ref[...]], axis=1)
    sin = jnp.concatenate([sin_ref[...], sin_ref[...]], axis=1)
    scale = HEAD_DIM ** -0.5 * LOG2_E

    def heads_out(ref, first, z):
        ones = jnp.ones((z.shape[0], HEAD_DIM), ref.dtype)
        for r in range(N_KV):
            zr = z[:, r * HEAD_DIM:(r + 1) * HEAD_DIM].astype(ref.dtype)
            ref[0, first + r] = zr if ref.shape[-1] == HEAD_DIM else jnp.concatenate([zr, ones], axis=1)

    for j in range(d // kd):
        z = _dot(xn, w_ref[:, j * kd:(j + 1) * kd])
        heads_out(q_ref, N_KV * j, _head_norm_rope(z, qn_ref[...], seg_ref, cos, sin) * scale)
    col = lambda c: _dot(xn, w_ref[:, d + c * kd:d + (c + 1) * kd])
    heads_out(kc_ref, 0, col(0))
    heads_out(vc_ref, 0, col(1))
    heads_out(ks_ref, 0, _head_norm_rope(col(2), kns_ref[...], seg_ref, cos, sin))
    heads_out(vs_ref, 0, col(3))
    heads_out(kw_ref, 0, _head_norm_rope(col(4), knw_ref[...], seg_ref, cos, sin))
    heads_out(vw_ref, 0, col(5))
    gate_ref[...] = jax.nn.sigmoid(_dot(xn, w_ref[:, d + 6 * kd:]))


def _nsa_proj(h, g_mix, w_in, cos_t, sin_t, q_norm, k_norm):
    bsz, t, d = h.shape
    n = bsz * t
    tm = ROW_TILE
    nt = t // tm
    kd = N_KV * HEAD_DIM
    n_heads = d // HEAD_DIM
    n_in = w_in.shape[1]
    w_pad = jnp.zeros((d, d + 6 * kd + LANES), F32).at[:, :n_in].set(w_in).astype(BF16)
    tile4 = lambda v: jnp.tile(v, kd // HEAD_DIM).reshape(1, kd)
    idx = jnp.arange(kd) // HEAD_DIM
    seg = (idx[:, None] == idx[None, :]).astype(BF16)
    head_spec = lambda nh, w=HEAD_DIM: pl.BlockSpec((1, nh, tm, w), lambda i: (i // nt, 0, i % nt, 0))
    head_shape = lambda nh, w=HEAD_DIM, dt=BF16: jax.ShapeDtypeStruct((bsz, nh, t, w), dt)
    flat_spec = lambda w: pl.BlockSpec((tm, w), lambda i: (i, 0))
    vw2 = 2 * HEAD_DIM
    return pl.pallas_call(
        _nsa_proj_kernel,
        grid=(n // tm,),
        in_specs=[flat_spec(d), _full((1, d)), _full(w_pad.shape), flat_spec(LANES), flat_spec(LANES),
                  _full((1, kd)), _full((1, kd)), _full((1, kd)), _full((kd, kd))],
        out_specs=[head_spec(n_heads), head_spec(N_KV), head_spec(N_KV), head_spec(N_KV), head_spec(N_KV, vw2),
                   head_spec(N_KV), head_spec(N_KV, vw2), flat_spec(LANES)],
        out_shape=[head_shape(n_heads), head_shape(N_KV, dt=F32), head_shape(N_KV, dt=F32),
                   head_shape(N_KV), head_shape(N_KV, vw2), head_shape(N_KV), head_shape(N_KV, vw2),
                   jax.ShapeDtypeStruct((n, LANES), F32)],
        compiler_params=_params(1),
        name="nsa_proj",
    )(h.reshape(n, d), g_mix.reshape(1, d), w_pad, cos_t, sin_t,
      tile4(q_norm), tile4(k_norm[1]), tile4(k_norm[2]), seg)


def _compress_kernel(xk_ref, xv_ref, pe_ref, w1_ref, w2_ref, kn_ref, cos_ref, sin_ref, o_ref):
    nch, dh = o_ref.shape[3], o_ref.shape[4]
    st = CMP_STRIDE
    hidden = w1_ref.shape[2]
    for which, x_ref in enumerate((xk_ref, xv_ref)):
        first = jnp.zeros((nch, hidden), F32)
        second = jnp.zeros((nch, hidden), F32)
        for tp in range(st):
            xt = x_ref[0, 0, pl.ds(tp, nch, stride=st), :]
            first = first + _dot((xt + pe_ref[which, tp:tp + 1, :]).astype(BF16),
                                 w1_ref[which, tp * dh:(tp + 1) * dh, :])
            second = second + _dot((xt + pe_ref[which, st + tp:st + tp + 1, :]).astype(BF16),
                                   w1_ref[which, (st + tp) * dh:(st + tp + 1) * dh, :])
        hid = first + pltpu.roll(second, nch - 1, 0)
        c = _dot(jax.nn.gelu(hid, approximate=True).astype(BF16), w2_ref[which])
        if which == 0:
            half = dh // 2
            cn = _rms(c, kn_ref[...])
            rot = jnp.concatenate([cn[:, half:], cn[:, :half]], axis=1)
            c = cn * cos_ref[0] + rot * sin_ref[0]
        o_ref[which, 0, 0] = c.astype(o_ref.dtype)


def _compress(kc_raw, vc_raw, pe, w1, w2, k_norm0, cos_t, sin_t):
    bsz, g, t, dh = kc_raw.shape
    st = CMP_STRIDE
    nch = t // st
    last = lambda tab: jnp.pad(tab.reshape(bsz, t, LANES)[:, CMP_BLOCK - 1::st, :dh], ((0, 0), (0, 1), (0, 0)))
    src = pl.BlockSpec((1, 1, t, dh), lambda b, gi: (b, gi, 0, 0))
    tab = pl.BlockSpec((1, nch, dh), lambda b, gi: (b, 0, 0))
    return pl.pallas_call(
        _compress_kernel,
        grid=(bsz, g),
        in_specs=[src, src, _full(pe.shape), _full(w1.shape), _full(w2.shape), _full((1, dh)), tab, tab],
        out_specs=pl.BlockSpec((2, 1, 1, nch, dh), lambda b, gi: (0, b, gi, 0, 0)),
        out_shape=jax.ShapeDtypeStruct((2, bsz, g, nch, dh), BF16),
        compiler_params=_params(2),
        name="nsa_compress",
    )(kc_raw, vc_raw, pe, w1.astype(BF16), w2.astype(BF16), k_norm0.reshape(1, dh), last(cos_t), last(sin_t))


def _attend_chunk(q, k, v, bias, m_ref, acc_ref, r_heads, tq_n):
    tk = k.shape[0]
    s = _dot_nt(q, k)
    m_all = m_ref[...]
    acc_all = acc_ref[...]
    m_out, scaled, probs = [], [], []
    for r in range(r_heads):
        rs = slice(r * tq_n, (r + 1) * tq_n)
        sr = s[rs] + bias
        m_old = m_all[rs]
        m_new = jnp.maximum(m_old, jnp.max(sr, axis=-1, keepdims=True))
        m_wide = m_new if tk == m_new.shape[1] else jnp.concatenate([m_new] * (tk // m_new.shape[1]), axis=1)
        probs.append(jnp.exp2(sr - m_wide).astype(BF16))
        m_out.append(m_new)
        scaled.append(jnp.exp2(m_old - m_new) * acc_all[rs])
    m_ref[...] = jnp.concatenate(m_out, axis=0)
    acc_ref[...] = jnp.concatenate(scaled, axis=0) + _dot(jnp.concatenate(probs, axis=0), v)


def _nsa_attn_kernel(q_ref, kc_ref, vc_ref, ks_ref, vs_ref, kw_ref, vw_ref, gate_ref, wsel_ref, e_ref, o_ref,
                     m_ref, acc_ref, og_ref, bias_ref, *, n_sel):
    g = pl.program_id(1)
    qi = pl.program_id(2)
    r_heads, tq_n, dh = q_ref.shape[1], q_ref.shape[2], q_ref.shape[3]
    rows = r_heads * tq_n
    t0 = qi * tq_n
    tk = K_TILE
    q = q_ref[0].reshape(rows, dh)
    head_rows = [slice(r * tq_n, (r + 1) * tq_n) for r in range(r_heads)]

    gates = gate_ref[0]
    gate_lane = lax.broadcasted_iota(I32, gates.shape, 1)

    def gate(r, branch):
        col = (g * r_heads + r) * 3 + branch
        return jnp.sum(jnp.where(gate_lane == col, gates, 0.0), axis=-1, keepdims=True)

    def restart():
        m_ref[...] = jnp.full(m_ref.shape, NEG, F32)
        acc_ref[...] = jnp.zeros(acc_ref.shape, F32)

    def add_branch(branch):
        acc = acc_ref[...]
        o = acc * (1.0 / pltpu.roll(acc, dh, 1))
        og_ref[...] += jnp.concatenate([gate(r, branch) * o[rs, :dh] for r, rs in enumerate(head_rows)], axis=0)

    kc = kc_ref[0, 0]
    nc = kc.shape[0]
    tq_c = t0 + lax.broadcasted_iota(I32, (tq_n, nc), 0)
    mask_c = CMP_STRIDE * lax.broadcasted_iota(I32, (tq_n, nc), 1) + (CMP_BLOCK - 1) <= tq_c
    s_c = _dot_nt(q, kc)
    imp = jnp.zeros((tq_n, nc), F32)
    probs = []
    for rs in head_rows:
        sr = jnp.where(mask_c, s_c[rs], NEG)
        e = jnp.where(mask_c, jnp.exp2(sr - jnp.max(sr, axis=-1, keepdims=True)), 0.0)
        l = jnp.sum(e, axis=-1, keepdims=True)
        p = e * jnp.where(l > 0.0, 1.0 / l, 0.0)
        imp = imp + p
        probs.append(p.astype(BF16))
    o_c = _dot(jnp.concatenate(probs, axis=0), vc_ref[0, 0])
    og_c = jnp.concatenate([gate(r, 0) * o_c[rs] for r, rs in enumerate(head_rows)], axis=0)

    imp_hi = imp.astype(BF16)
    imp_lo = (imp - imp_hi.astype(F32)).astype(BF16)
    p_slc = _dot_nt(wsel_ref[...], imp_hi) + _dot_nt(wsel_ref[...], imp_lo)
    nsb = p_slc.shape[0]
    blk = lax.broadcasted_iota(I32, (nsb, tq_n), 0)
    tq_t = t0 + lax.broadcasted_iota(I32, (nsb, tq_n), 1)
    cur = tq_t // SLC_BLOCK
    forced = (blk == 0) | (blk == cur) | (blk == cur - 1)
    score = jnp.where(forced, jnp.inf, jnp.where(blk * SLC_BLOCK <= tq_t, p_slc, -jnp.inf))
    sub = CONV_SUBLANES
    groups = [slice(g0, g0 + sub) for g0 in range(0, nsb, sub)]
    cnt = [jnp.zeros((sub, tq_n), F32) for _ in groups]
    for i in range(nsb):
        ri = score[i:i + 1, :]
        for gi, rows_g in enumerate(groups):
            sg = score[rows_g]
            if rows_g.start > i:
                beats = ri >= sg
            elif rows_g.stop <= i:
                beats = ri > sg
            else:
                idx = rows_g.start + lax.broadcasted_iota(I32, (sub, tq_n), 0)
                beats = (ri > sg) | ((ri == sg) & (idx > i))
            cnt[gi] = cnt[gi] + jnp.where(beats, 1.0, 0.0)
    dropped = jnp.where(jnp.concatenate(cnt, axis=0) < n_sel, 0.0, 1.0)
    dropped = jnp.concatenate([dropped, jnp.zeros((LANES - nsb, tq_n), F32)], axis=0).T.astype(BF16)

    wq = min(tq_n, WINDOW_Q_TILE)
    kwn = WINDOW + wq
    gates_w = [gate(r, 2) for r in range(r_heads)]
    pieces = [[] for _ in head_rows]
    for h in range(tq_n // wq):
        q_h = jnp.concatenate([q[r * tq_n + h * wq:r * tq_n + (h + 1) * wq] for r in range(r_heads)], axis=0)
        start = pl.multiple_of(jnp.maximum(t0 + h * wq - WINDOW, 0), wq)
        kpos_w = start + lax.broadcasted_iota(I32, (wq, kwn), 1)
        tq_w = t0 + h * wq + lax.broadcasted_iota(I32, (wq, kwn), 0)
        bias_w = jnp.where((kpos_w <= tq_w) & (kpos_w > tq_w - WINDOW), 0.0, NEG)
        s_w = _dot_nt(q_h, kw_ref[0, 0, pl.ds(start, kwn), :])
        probs = []
        for r in range(r_heads):
            sr = s_w[r * wq:(r + 1) * wq] + bias_w
            probs.append(jnp.exp2(sr - jnp.max(sr, axis=-1, keepdims=True)).astype(BF16))
        acc_w = _dot(jnp.concatenate(probs, axis=0), vw_ref[0, 0, pl.ds(start, kwn), :])
        o_w = acc_w * (1.0 / pltpu.roll(acc_w, dh, 1))
        for r in range(r_heads):
            pieces[r].append(gates_w[r][h * wq:(h + 1) * wq] * o_w[r * wq:(r + 1) * wq, :dh])
    og_ref[...] = og_c + jnp.concatenate([piece for per_head in pieces for piece in per_head], axis=0)

    t_keys = e_ref.shape[1]
    kpos = lax.broadcasted_iota(I32, (tq_n, t_keys), 1)
    tq_s = t0 + lax.broadcasted_iota(I32, (tq_n, t_keys), 0)
    bias_all = jnp.where(kpos <= tq_s, _dot(dropped, e_ref[...]), NEG)
    for c in range(t_keys // tk):
        bias_ref[c] = bias_all[:, c * tk:(c + 1) * tk]
    restart()

    def sel_step(c, carry):
        k0 = pl.multiple_of(c * tk, tk)
        _attend_chunk(q, ks_ref[0, 0, pl.ds(k0, tk), :], vs_ref[0, 0, pl.ds(k0, tk), :], bias_ref[c],
                      m_ref, acc_ref, r_heads, tq_n)
        return carry

    lax.fori_loop(0, (t0 + tq_n + tk - 1) // tk, sel_step, 0)
    add_branch(1)

    for r, rs in enumerate(head_rows):
        o_ref[0, :, r * dh:(r + 1) * dh] = og_ref[rs].astype(o_ref.dtype)


def _nsa_attn(q, kc, vc, ks, vs, kw, vw, gate, n_sel):
    bsz, n_heads, t, dh = q.shape
    g = N_KV
    r_heads = n_heads // g
    tq = Q_TILE
    nch = kc.shape[2]
    a = SLC_BLOCK // CMP_STRIDE
    bb = CMP_BLOCK // CMP_STRIDE
    nsb = t // SLC_BLOCK
    j = jnp.arange(nsb)[:, None]
    c = jnp.arange(nch)[None, :]
    wsel = sum(((c == a * j + m + n_) & (c < nch - 1)).astype(F32)
               for m in range(a) for n_ in range(bb)).astype(BF16)
    tk = K_TILE
    key_blk = (jnp.arange(t) // SLC_BLOCK).reshape(1, t)
    e = jnp.where(key_blk == jnp.arange(LANES).reshape(LANES, 1), NEG, 0.0).astype(BF16)
    rows = r_heads * tq
    kv_spec = lambda n_rows, w=dh: pl.BlockSpec((1, 1, n_rows, w), lambda b, gi, qi: (b, gi, 0, 0))
    return pl.pallas_call(
        functools.partial(_nsa_attn_kernel, n_sel=n_sel),
        grid=(bsz, g, t // tq),
        in_specs=[
            pl.BlockSpec((1, r_heads, tq, dh), lambda b, gi, qi: (b, gi, qi, 0)),
            kv_spec(nch), kv_spec(nch), kv_spec(t), kv_spec(t, 2 * dh), kv_spec(t), kv_spec(t, 2 * dh),
            pl.BlockSpec((1, tq, LANES), lambda b, gi, qi: (b, qi, 0)),
            pl.BlockSpec((nsb, nch), lambda b, gi, qi: (0, 0)),
            pl.BlockSpec(e.shape, lambda b, gi, qi: (0, 0)),
        ],
        out_specs=pl.BlockSpec((1, tq, r_heads * dh), lambda b, gi, qi: (b, qi, gi)),
        out_shape=jax.ShapeDtypeStruct((bsz, t, n_heads * dh), BF16),
        scratch_shapes=[pltpu.VMEM((rows, 2 * dh), F32), pltpu.VMEM((rows, 2 * dh), F32),
                        pltpu.VMEM((rows, dh), F32), pltpu.VMEM((t // tk, tq, tk), F32)],
        compiler_params=_params(3),
        name="nsa_attention",
    )(q, kc, vc, ks, vs, kw, vw, gate, wsel, e)


def _out_mix_kernel(o_ref, w_ref, h_ref, gffn_ref, wcat_ref, br_ref, tri_ref,
                    h2_ref, xn_ref, slab_ref, slabt_ref, cnt_ref, carry_ref):
    @pl.when(pl.program_id(0) == 0)
    def _():
        carry_ref[...] = jnp.zeros_like(carry_ref)

    h = h_ref[...] + _dot(o_ref[...], w_ref[...])
    h2_ref[...] = h
    xn, slab = _route(h, gffn_ref[...], wcat_ref, br_ref, tri_ref, carry_ref)
    xn_ref[...] = xn
    slab_ref[...] = slab
    slabt_ref[...] = slab.T[:ROUTE_FIELDS]
    cnt_ref[...] = carry_ref[...]


def _out_mix(o2, w_out, h2, g_ffn, w_rg, b_rg, w_re, b_re):
    n, d = h2.shape
    tm = ROW_TILE
    wcat, br, tri = _router_operands(w_rg, b_rg, w_re, b_re, tm)
    row = pl.BlockSpec((tm, d), lambda i: (i, 0))
    return pl.pallas_call(
        _out_mix_kernel,
        grid=(n // tm,),
        in_specs=[row, _full((d, d)), row, _full((1, d)), _full((d, 2 * LANES)), _full((1, LANES)), _full((tm, tm))],
        out_specs=[row, row, pl.BlockSpec((tm, LANES), lambda i: (i, 0)),
                   pl.BlockSpec((ROUTE_FIELDS, tm), lambda i: (0, i)), _full((1, LANES))],
        out_shape=[jax.ShapeDtypeStruct((n, d), F32), jax.ShapeDtypeStruct((n, d), F32),
                   jax.ShapeDtypeStruct((n, LANES), F32), jax.ShapeDtypeStruct((ROUTE_FIELDS, n), F32),
                   jax.ShapeDtypeStruct((1, LANES), F32)],
        scratch_shapes=[pltpu.VMEM((1, LANES), F32)],
        compiler_params=_params(1),
        name="nsa_out_mix",
    )(o2, w_out.astype(BF16), h2, g_ffn.reshape(1, d), wcat, br, tri)


def _nsa_layer(h, positions, g_mix, w_in, q_norm, k_norm, cmp_pe, cmp_w1, cmp_w2):
    bsz, t, d = h.shape
    cos_t, sin_t = _rope_table(positions)
    q, kc_raw, vc_raw, ks, vs, kw, vw, gate = _nsa_proj(h, g_mix, w_in, cos_t, sin_t, q_norm, k_norm)
    cmp = _compress(kc_raw, vc_raw, cmp_pe, cmp_w1, cmp_w2, k_norm[0], cos_t, sin_t)
    n_sel = min(N_SEL, t // SLC_BLOCK)
    return _nsa_attn(q, cmp[0], cmp[1], ks, vs, kw, vw, gate.reshape(bsz, t, LANES), n_sel)


def kernel(x, p, positions, g_mix, g_ffn, g_ple, conv_w_pw1, conv_b_pw1, conv_w_dw, conv_b_dw, conv_ln_g, conv_ln_b, conv_w_pw2, conv_b_pw2, nsa_w_in, nsa_q_norm, nsa_k_norm, nsa_cmp_pe, nsa_cmp_w1, nsa_cmp_w2, nsa_w_out, moe_w_rg, moe_b_rg, moe_w_re, moe_b_re, moe_w_gate, moe_w_up, moe_w_down, ple_w_proj, ple_w_gate):
    bsz, t, d = x.shape
    n = bsz * t
    depth = p.shape[0]
    h = x
    p3 = p.reshape(depth, n, p.shape[-1])
    for i in range(depth):
        j = i // 2
        route_w = (g_ffn[i], moe_w_rg[i], moe_b_rg[i], moe_w_re[i], moe_b_re[i])
        if i % 2 == 0:
            u = _pw1_glu(h.reshape(n, d), g_mix[i], conv_w_pw1[j], conv_b_pw1[j])
            h, xn, slab, slab_t, counts = _conv_mix(u.reshape(bsz, t, d), h, conv_w_dw[j], conv_b_dw[j],
                                                    conv_ln_g[j], conv_ln_b[j], conv_w_pw2[j], conv_b_pw2[j], *route_w)
        else:
            o = _nsa_layer(h, positions, g_mix[i], nsa_w_in[j], nsa_q_norm[j], nsa_k_norm[j],
                           nsa_cmp_pe[j], nsa_cmp_w1[j], nsa_cmp_w2[j])
            h, xn, slab, slab_t, counts = _out_mix(o.reshape(n, d), nsa_w_out[j], h.reshape(n, d), *route_w)
        h = _moe_ple(h.reshape(n, d), xn, slab, slab_t, counts, i, p3, g_ple[i],
                     moe_w_gate, moe_w_up, moe_w_down, ple_w_gate[i], ple_w_proj[i]).reshape(bsz, t, d)
    return h
```

```python
import functools

import jax
import jax.numpy as jnp
from jax import lax
from jax.experimental import pallas as pl
from jax.experimental.pallas import tpu as pltpu

F32 = jnp.float32
BF16 = jnp.bfloat16
I32 = jnp.int32

LANES = 128
VMEM_LIMIT = 56 * 1024 * 1024

EPS = 1e-6
NEG = -1e30
LOG2_E = 1.4426950408889634

CONV_WIDTH = 31
HEAD_DIM = 64
N_KV = 4
CMP_BLOCK = 32
CMP_STRIDE = 16
SLC_BLOCK = 64
N_SEL = 16
WINDOW = 512
ROPE_THETA = 10000.0
N_GROUPS = 4
EXPERTS_PER_GROUP = 8
N_EXPERTS = N_GROUPS * EXPERTS_PER_GROUP

ROW_TILE = 512
CONV_CHUNK = 64
CONV_ROWS = 128
CONV_SUBLANES = 8
CONV_HALO = 32
EXPERT_ROWS = 256
Q_TILE = 512
K_TILE = 512
WINDOW_Q_TILE = 128
ROUTE_FIELDS = 8


def _params(n_grid):
    return pltpu.CompilerParams(dimension_semantics=("arbitrary",) * n_grid,
                                vmem_limit_bytes=VMEM_LIMIT)


def _dot(a, b):
    return jnp.dot(a, b, preferred_element_type=F32)


def _dot_nt(a, b):
    return lax.dot_general(a, b, (((1,), (1,)), ((), ())), preferred_element_type=F32)


def _rms(x, g):
    return x * lax.rsqrt(jnp.mean(x * x, axis=-1, keepdims=True) + EPS) * g


def _full(shape):
    n = len(shape)
    return pl.BlockSpec(shape, lambda *_: (0,) * n)


def _route(h, g_ffn, wcat_ref, br_ref, tri_ref, carry_ref):
    xn = _rms(h, g_ffn)
    hi = xn.astype(BF16)
    lo = (xn - hi.astype(F32)).astype(BF16)
    r1 = _dot(hi, wcat_ref[...])
    r2 = _dot(lo, wcat_ref[:, :LANES])
    logits = r1[:, :LANES] + r1[:, LANES:] + r2 + br_ref[...]

    lane = lax.broadcasted_iota(I32, logits.shape, 1).astype(F32)
    lg = jnp.where(lane < N_GROUPS, logits, -jnp.inf)
    m = jnp.max(lg, axis=-1, keepdims=True)
    grp = jnp.min(jnp.where(lg == m, lane, float(LANES)), axis=-1, keepdims=True)
    pg_sel = 1.0 / jnp.sum(jnp.exp(lg - m), axis=-1, keepdims=True)

    lo_e = N_GROUPS + EXPERTS_PER_GROUP * grp
    le = jnp.where((lane >= lo_e) & (lane < lo_e + EXPERTS_PER_GROUP), logits, -jnp.inf)
    v1 = jnp.max(le, axis=-1, keepdims=True)
    i1 = jnp.min(jnp.where(le == v1, lane, float(LANES)), axis=-1, keepdims=True)
    le2 = jnp.where(lane == i1, -jnp.inf, le)
    v2 = jnp.max(le2, axis=-1, keepdims=True)
    i2 = jnp.min(jnp.where(le2 == v2, lane, float(LANES)), axis=-1, keepdims=True)
    e21 = jnp.exp(v2 - v1)
    w1 = pg_sel / (1.0 + e21)
    w2 = pg_sel * e21 / (1.0 + e21)
    e1 = i1 - N_GROUPS
    e2 = i2 - N_GROUPS

    oh1 = jnp.where(lane == e1, 1.0, 0.0)
    oh2 = jnp.where(lane == e2, 1.0, 0.0)
    c1 = _dot(tri_ref[...], oh1.astype(BF16))
    c2 = _dot(tri_ref[...], oh2.astype(BF16))
    carry = carry_ref[...]
    tot1 = jnp.sum(oh1, axis=0, keepdims=True)
    tot2 = jnp.sum(oh2, axis=0, keepdims=True)
    rank1 = jnp.sum(oh1 * (c1 + carry), axis=-1, keepdims=True)
    rank2 = jnp.sum(oh2 * (c2 + carry + tot1), axis=-1, keepdims=True)
    carry_ref[...] = carry + tot1 + tot2

    slab = jnp.where(lane == 0, e1, jnp.where(lane == 1, e2, jnp.where(lane == 2, w1, jnp.where(
        lane == 3, w2, jnp.where(lane == 4, rank1, jnp.where(lane == 5, rank2, 0.0))))))
    return xn, slab


def _router_operands(w_rg, b_rg, w_re, b_re, tm):
    d = w_rg.shape[0]
    w = jnp.zeros((d, LANES), F32).at[:, :N_GROUPS].set(w_rg).at[:, N_GROUPS:N_GROUPS + N_EXPERTS].set(w_re)
    w_hi = w.astype(BF16)
    w_lo = (w - w_hi.astype(F32)).astype(BF16)
    wcat = jnp.concatenate([w_hi, w_lo], axis=1)
    br = jnp.zeros((1, LANES), F32).at[0, :N_GROUPS].set(b_rg).at[0, N_GROUPS:N_GROUPS + N_EXPERTS].set(b_re)
    r = lax.broadcasted_iota(I32, (tm, tm), 0)
    c = lax.broadcasted_iota(I32, (tm, tm), 1)
    tri = (c < r).astype(BF16)
    return wcat, br, tri


def _pw1_glu_kernel(x_ref, g_ref, w_ref, b_ref, o_ref, *, chunk):
    xn = _rms(x_ref[...], g_ref[...]).astype(BF16)
    d = o_ref.shape[1]
    for j in range(d // chunk):
        sa = slice(j * chunk, (j + 1) * chunk)
        sg = slice(d + j * chunk, d + (j + 1) * chunk)
        a = _dot(xn, w_ref[:, sa]) + b_ref[:, sa]
        g = _dot(xn, w_ref[:, sg]) + b_ref[:, sg]
        o_ref[:, sa] = a * jax.nn.sigmoid(g)


def _pw1_glu(x2, g, w, b):
    n, d = x2.shape
    tm = ROW_TILE
    return pl.pallas_call(
        functools.partial(_pw1_glu_kernel, chunk=512),
        grid=(n // tm,),
        in_specs=[pl.BlockSpec((tm, d), lambda i: (i, 0)), _full((1, d)), _full((d, 2 * d)), _full((1, 2 * d))],
        out_specs=pl.BlockSpec((tm, d), lambda i: (i, 0)),
        out_shape=jax.ShapeDtypeStruct((n, d), F32),
        compiler_params=_params(1),
        name="pw1_glu",
    )(x2, g.reshape(1, d), w.astype(BF16), b.reshape(1, 2 * d))


def _conv_mix_kernel(u_ref, halo_ref, x_ref, wdw_ref, bdw_ref, lng_ref, lnb_ref, w2_ref, b2_ref,
                     gffn_ref, wcat_ref, br_ref, tri_ref,
                     h_ref, xn_ref, slab_ref, slabt_ref, cnt_ref,
                     ext_ref, conv_ref, act_ref, carry_ref):
    b = pl.program_id(0)
    i = pl.program_id(1)
    tt = u_ref.shape[1]

    @pl.when((b == 0) & (i == 0))
    def _():
        carry_ref[...] = jnp.zeros_like(carry_ref)

    halo = halo_ref[0]
    ext_ref[0:CONV_HALO, :] = jnp.where(i == 0, jnp.zeros_like(halo), halo)
    ext_ref[CONV_HALO:CONV_HALO + tt, :] = u_ref[0]
    ext_ref[CONV_HALO + tt:, :] = jnp.zeros((CONV_SUBLANES, ext_ref.shape[1]), F32)
    lead = CONV_HALO - (CONV_WIDTH - 1)

    sub = CONV_SUBLANES
    for lt in range(ext_ref.shape[1] // LANES):
        cols = slice(lt * LANES, (lt + 1) * LANES)

        def conv_block(c, carry, cols=cols):
            base = pl.multiple_of(c * CONV_ROWS, CONV_ROWS)
            acc = jnp.zeros((CONV_ROWS, LANES), F32)
            for rem in range(sub):
                part = None
                for a in range(CONV_HALO // sub + 1):
                    k = rem + sub * a - lead
                    if 0 <= k < CONV_WIDTH:
                        term = wdw_ref[k:k + 1, cols] * ext_ref[pl.ds(base + sub * a, CONV_ROWS + sub), cols]
                        part = term if part is None else part + term
                acc = acc + part[rem:rem + CONV_ROWS]
            conv_ref[pl.ds(base, CONV_ROWS), cols] = acc
            return carry

        lax.fori_loop(0, tt // CONV_ROWS, conv_block, 0)

    def chunk(c, carry):
        base = pl.multiple_of(c * CONV_CHUNK, CONV_CHUNK)
        acc = conv_ref[pl.ds(base, CONV_CHUNK), :] + bdw_ref[...]
        mu = jnp.mean(acc, axis=-1, keepdims=True)
        cen = acc - mu
        var = jnp.mean(cen * cen, axis=-1, keepdims=True)
        y = cen * lax.rsqrt(var + EPS) * lng_ref[...] + lnb_ref[...]
        act_ref[pl.ds(base, CONV_CHUNK), :] = (y * jax.nn.sigmoid(y)).astype(BF16)
        return carry

    lax.fori_loop(0, tt // CONV_CHUNK, chunk, 0)

    h = x_ref[0] + _dot(act_ref[...], w2_ref[...]) + b2_ref[...]
    h_ref[0] = h
    xn, slab = _route(h, gffn_ref[...], wcat_ref, br_ref, tri_ref, carry_ref)
    xn_ref[...] = xn
    slab_ref[...] = slab
    slabt_ref[...] = slab.T[:ROUTE_FIELDS]
    cnt_ref[...] = carry_ref[...]


def _conv_mix(u, x, w_dw, b_dw, ln_g, ln_b, w_pw2, b_pw2, g_ffn, w_rg, b_rg, w_re, b_re):
    bsz, t, d = x.shape
    tt = ROW_TILE
    nt = t // tt
    n = bsz * t
    wcat, br, tri = _router_operands(w_rg, b_rg, w_re, b_re, tt)
    wdw = jnp.zeros((CONV_HALO, d), F32).at[:CONV_WIDTH].set(w_dw)
    hb = tt // CONV_HALO
    row = lambda v: v.reshape(1, d)
    return pl.pallas_call(
        _conv_mix_kernel,
        grid=(bsz, nt),
        in_specs=[
            pl.BlockSpec((1, tt, d), lambda b, i: (b, i, 0)),
            pl.BlockSpec((1, CONV_HALO, d), lambda b, i: (b, jnp.maximum(i * hb - 1, 0), 0)),
            pl.BlockSpec((1, tt, d), lambda b, i: (b, i, 0)),
            _full((CONV_HALO, d)), _full((1, d)), _full((1, d)), _full((1, d)),
            _full((d, d)), _full((1, d)), _full((1, d)),
            _full((d, 2 * LANES)), _full((1, LANES)), _full((tt, tt)),
        ],
        out_specs=[
            pl.BlockSpec((1, tt, d), lambda b, i: (b, i, 0)),
            pl.BlockSpec((tt, d), lambda b, i: (b * nt + i, 0)),
            pl.BlockSpec((tt, LANES), lambda b, i: (b * nt + i, 0)),
            pl.BlockSpec((ROUTE_FIELDS, tt), lambda b, i: (0, b * nt + i)),
            _full((1, LANES)),
        ],
        out_shape=[
            jax.ShapeDtypeStruct((bsz, t, d), F32),
            jax.ShapeDtypeStruct((n, d), F32),
            jax.ShapeDtypeStruct((n, LANES), F32),
            jax.ShapeDtypeStruct((ROUTE_FIELDS, n), F32),
            jax.ShapeDtypeStruct((1, LANES), F32),
        ],
        scratch_shapes=[pltpu.VMEM((CONV_HALO + tt + CONV_SUBLANES, d), F32), pltpu.VMEM((tt, d), F32),
                        pltpu.VMEM((tt, d), BF16),
                        pltpu.VMEM((1, LANES), F32)],
        compiler_params=_params(2),
        name="conv_mix",
    )(u, u, x, wdw, row(b_dw), row(ln_g), row(ln_b), w_pw2.astype(BF16), row(b_pw2), row(g_ffn), wcat, br, tri)


def _invert_slot_map(dest_ref, meta_ref, asg_ref):
    n_asg = dest_ref.shape[0]
    groups = meta_ref.shape[0] // 3

    def pad_group(e, carry):
        off = meta_ref[2 * groups + e]

        def body(s, c):
            asg_ref[s] = s + off
            return c

        lax.fori_loop(meta_ref[e], meta_ref[groups + e], body, 0)
        return carry

    lax.fori_loop(0, groups, pad_group, 0)

    def real(a, carry):
        asg_ref[dest_ref[a]] = a
        return carry

    lax.fori_loop(0, n_asg, real, 0, unroll=8)


def _expert_fused_kernel(be_ref, nu_ref, dest_ref, meta_ref, xn_hbm, wg_ref, wu_ref, wd_ref, y_hbm,
                         asg_ref, xbuf, ybuf, wgb_ref, wub_ref, wdb_ref, gsem, ssem, *, n_tok):
    i = pl.program_id(0)
    nu = nu_ref[0]
    be = xbuf.shape[1]
    slot = i % 2
    other = 1 - slot

    @pl.when(i == 0)
    def _():
        _invert_slot_map(dest_ref, meta_ref, asg_ref)

    def gather_copy(blk, buf, r):
        a = asg_ref[blk * be + r]
        tok = a & (n_tok - 1) if n_tok & (n_tok - 1) == 0 else lax.rem(a, n_tok)
        return pltpu.make_async_copy(xn_hbm.at[pl.ds(tok, 1)], xbuf.at[buf, pl.ds(r, 1)], gsem.at[buf])

    def scatter_copy(blk, buf, r):
        return pltpu.make_async_copy(ybuf.at[buf, pl.ds(r, 1)], y_hbm.at[pl.ds(asg_ref[blk * be + r], 1)],
                                     ssem.at[buf])

    def for_rows(fn):
        def body(r, carry):
            fn(r)
            return carry

        lax.fori_loop(0, be, body, 0, unroll=32)

    @pl.when(i == 0)
    def _():
        for_rows(lambda r: gather_copy(0, 0, r).start())
        ybuf[1] = jnp.zeros(ybuf.shape[1:], ybuf.dtype)

    def expert_step(cur, oth):
        for_rows(lambda r: gather_copy(0, cur, r).wait())

        @pl.when((i == 0) | (be_ref[i] != be_ref[jnp.maximum(i - 1, 0)]))
        def _():
            wgb_ref[...] = wg_ref[0, 0].astype(BF16)
            wub_ref[...] = wu_ref[0, 0].astype(BF16)
            wdb_ref[...] = wd_ref[0, 0].astype(BF16)

        nxt = jnp.minimum(i + 1, nu - 1)
        prv = jnp.maximum(i - 1, 0)
        for r in range(be):
            gather_copy(nxt, oth, r).start(priority=r % 2)
            scatter_copy(prv, oth, r).start(priority=(r + 1) % 2)
        x = xbuf[cur].astype(BF16)
        a = _dot(x, wgb_ref[...])
        u = _dot(x, wub_ref[...])
        hid = (a * jax.nn.sigmoid(a) * u).astype(BF16)
        ybuf[cur] = _dot(hid, wdb_ref[...])
        for_rows(lambda r: scatter_copy(0, oth, r).wait())

    for half in (0, 1):
        pl.when((i < nu) & (slot == half))(functools.partial(expert_step, half, 1 - half))

    @pl.when(i == nu)
    def _():
        for_rows(lambda r: gather_copy(0, slot, r).wait())
        for_rows(lambda r: scatter_copy(nu - 1, other, r).start())
        for_rows(lambda r: scatter_copy(0, other, r).wait())
        xbuf[slot] = jnp.zeros(xbuf.shape[1:], xbuf.dtype)

        def zero_copy(blk):
            return pltpu.make_async_copy(xbuf.at[slot], y_hbm.at[pl.ds(pl.multiple_of(blk * be, be), be)],
                                         gsem.at[slot])

        def start(blk, carry):
            zero_copy(blk).start()
            return carry

        def finish(blk, carry):
            zero_copy(blk).wait()
            return carry

        lax.fori_loop(nu, pl.num_programs(0), start, 0)
        lax.fori_loop(nu, pl.num_programs(0), finish, 0)


def _experts_fused(blk_e, n_used, dest_flat, meta, n_slots, xn, layer, w_gate, w_up, w_down):
    n, d = xn.shape
    f = w_gate.shape[3]
    be = EXPERT_ROWS
    wspec = lambda a, b: pl.BlockSpec((1, 1, a, b), lambda i, e, nu, dest, meta: (layer, e[i], 0, 0))
    return pl.pallas_call(
        functools.partial(_expert_fused_kernel, n_tok=n),
        grid_spec=pltpu.PrefetchScalarGridSpec(
            num_scalar_prefetch=4,
            grid=(n_slots // be,),
            in_specs=[pl.BlockSpec(memory_space=pl.ANY), wspec(d, f), wspec(d, f), wspec(f, d)],
            out_specs=pl.BlockSpec(memory_space=pl.ANY),
            scratch_shapes=[pltpu.SMEM((n_slots,), I32),
                            pltpu.VMEM((2, be, d), F32), pltpu.VMEM((2, be, d), F32),
                            pltpu.VMEM((d, f), BF16), pltpu.VMEM((d, f), BF16), pltpu.VMEM((f, d), BF16),
                            pltpu.SemaphoreType.DMA((2,)), pltpu.SemaphoreType.DMA((2,))],
        ),
        out_shape=jax.ShapeDtypeStruct((n_slots, d), F32),
        compiler_params=_params(1),
        name="moe_experts",
    )(blk_e, n_used, dest_flat, meta, xn, w_gate, w_up, w_down)


def _ple_kernel(h_ref, y0_ref, y1_ref, slab_ref, p_ref, gple_ref, wg_ref, wp_ref, o_ref):
    slab = slab_ref[...]
    lane = lax.broadcasted_iota(I32, slab.shape, 1)
    w0 = jnp.sum(jnp.where(lane == 2, slab, 0.0), axis=-1, keepdims=True)
    w1 = jnp.sum(jnp.where(lane == 3, slab, 0.0), axis=-1, keepdims=True)
    hm = h_ref[...] + w0 * y0_ref[...] + w1 * y1_ref[...]
    gate = jax.nn.sigmoid(_dot(_rms(hm, gple_ref[...]).astype(BF16), wg_ref[...]))
    o_ref[...] = hm + gate * _dot(p_ref[0].astype(BF16), wp_ref[...])


def _ple(h2, y, slab, layer, p3, g_ple, w_gate, w_proj):
    n, d = h2.shape
    dp = p3.shape[2]
    tm = ROW_TILE
    nt = n // tm
    row = lambda w: pl.BlockSpec((tm, w), lambda i: (i, 0))
    return pl.pallas_call(
        _ple_kernel,
        grid=(nt,),
        in_specs=[row(d), row(d), pl.BlockSpec((tm, d), lambda i: (nt + i, 0)), row(LANES),
                  pl.BlockSpec((1, tm, dp), lambda i: (layer, i, 0)), _full((1, d)), _full((d, d)), _full((dp, d))],
        out_specs=row(d),
        out_shape=jax.ShapeDtypeStruct((n, d), F32),
        compiler_params=_params(1),
        name="moe_combine_ple",
    )(h2, y, y, slab, p3, g_ple.reshape(1, d), w_gate.astype(BF16), w_proj.astype(BF16))


def _moe_ple(h2, xn, slab, slab_t, counts, layer, p3, g_ple, w_gate_e, w_up_e, w_down_e, ple_w_gate, ple_w_proj):
    n, d = h2.shape
    be = EXPERT_ROWS
    n_blocks = (2 * n) // be + N_EXPERTS
    cnt = counts[0, :N_EXPERTS].astype(I32)
    padded = (cnt + be - 1) // be * be
    ends = jnp.cumsum(padded)
    starts = ends - padded
    eid = slab_t[0:2].astype(I32)
    rank = slab_t[4:6].astype(I32)
    expert_ids = jnp.arange(N_EXPERTS, dtype=I32).reshape(N_EXPERTS, 1, 1)
    slot_base = jnp.sum(jnp.where(eid[None] == expert_ids, starts.reshape(N_EXPERTS, 1, 1), 0), axis=0)
    dest_flat = (slot_base + rank).reshape(2 * n)
    block_start = jnp.arange(n_blocks, dtype=I32) * be
    blk_e = jnp.minimum(jnp.sum((ends[None, :] <= block_start[:, None]).astype(I32), axis=1), N_EXPERTS - 1)
    n_used = (ends[-1:] // be).astype(I32)
    seen = jnp.cumsum(cnt)
    pad_lo = jnp.concatenate([starts + cnt, ends[-1:]])
    pad_hi = jnp.concatenate([ends, jnp.full((1,), n_blocks * be, I32)])
    pad_off = jnp.concatenate([2 * n - seen, jnp.zeros((1,), I32)])
    meta = jnp.concatenate([pad_lo, pad_hi, pad_off]).astype(I32)
    y = _experts_fused(blk_e, n_used, dest_flat, meta, n_blocks * be, xn, layer, w_gate_e, w_up_e, w_down_e)
    return _ple(h2, y, slab, layer, p3, g_ple, ple_w_gate, ple_w_proj)


def _rope_table_kernel(pos_ref, inv_ref, sign_ref, cos_ref, sin_ref):
    ang = pos_ref[...] * inv_ref[...]
    cos_ref[...] = jnp.cos(ang)
    sin_ref[...] = jnp.sin(ang) * sign_ref[...]


def _rope_table(positions):
    n = positions.size
    tm = ROW_TILE
    half = HEAD_DIM // 2
    lane = jnp.arange(LANES)
    inv = 1.0 / (ROPE_THETA ** ((lane % half).astype(F32) / half))
    sign = jnp.where(lane % HEAD_DIM < half, -1.0, 1.0).astype(F32)
    return pl.pallas_call(
        _rope_table_kernel,
        grid=(n // tm,),
        in_specs=[pl.BlockSpec((tm, 1), lambda i: (i, 0)), _full((1, LANES)), _full((1, LANES))],
        out_specs=[pl.BlockSpec((tm, LANES), lambda i: (i, 0))] * 2,
        out_shape=[jax.ShapeDtypeStruct((n, LANES), F32)] * 2,
        compiler_params=_params(1),
        name="rope_table",
    )(positions.astype(F32).reshape(n, 1), inv.reshape(1, LANES), sign.reshape(1, LANES))


def _head_norm_rope(z, gain, seg_ref, cos, sin):
    z2 = z * z
    hi = z2.astype(BF16)
    lo = (z2 - hi.astype(F32)).astype(BF16)
    ssq = _dot(hi, seg_ref[...]) + _dot(lo, seg_ref[...])
    zn = z * lax.rsqrt(ssq * (1.0 / HEAD_DIM) + EPS) * gain
    width = z.shape[1]
    half = HEAD_DIM // 2
    lane = lax.broadcasted_iota(I32, z.shape, 1)
    rot = jnp.where((lane & half) == 0, pltpu.roll(zn, width - half, 1), pltpu.roll(zn, half, 1))
    return zn * cos + rot * sin


def _nsa_proj_kernel(h_ref, g_ref, w_ref, cos_ref, sin_ref, qn_ref, kns_ref, knw_ref, seg_ref,
                     q_ref, kc_ref, vc_ref, ks_ref, vs_ref, kw_ref, vw_ref, gate_ref):
    d = h_ref.shape[1]
    kd = N_KV * HEAD_DIM
    xn = _rms(h_ref[...], g_ref[...]).astype(BF16)
    cos = jnp.concatenate([cos_ref[...], cos_ref[...]], axis=1)
    sin = jnp.concatenate([sin_ref[...], sin_ref[...]], axis=1)
    scale = HEAD_DIM ** -0.5 * LOG2_E

    def heads_out(ref, first, z):
        ones = jnp.ones((z.shape[0], HEAD_DIM), ref.dtype)
        for r in range(N_KV):
            zr = z[:, r * HEAD_DIM:(r + 1) * HEAD_DIM].astype(ref.dtype)
            ref[0, first + r] = zr if ref.shape[-1] == HEAD_DIM else jnp.concatenate([zr, ones], axis=1)

    for j in range(d // kd):
        z = _dot(xn, w_ref[:, j * kd:(j + 1) * kd])
        heads_out(q_ref, N_KV * j, _head_norm_rope(z, qn_ref[...], seg_ref, cos, sin) * scale)
    col = lambda c: _dot(xn, w_ref[:, d + c * kd:d + (c + 1) * kd])
    heads_out(kc_ref, 0, col(0))
    heads_out(vc_ref, 0, col(1))
    heads_out(ks_ref, 0, _head_norm_rope(col(2), kns_ref[...], seg_ref, cos, sin))
    heads_out(vs_ref, 0, col(3))
    heads_out(kw_ref, 0, _head_norm_rope(col(4), knw_ref[...], seg_ref, cos, sin))
    heads_out(vw_ref, 0, col(5))
    gate_ref[...] = jax.nn.sigmoid(_dot(xn, w_ref[:, d + 6 * kd:]))


def _nsa_proj(h, g_mix, w_in, cos_t, sin_t, q_norm, k_norm):
    bsz, t, d = h.shape
    n = bsz * t
    tm = ROW_TILE
    nt = t // tm
    kd = N_KV * HEAD_DIM
    n_heads = d // HEAD_DIM
    n_in = w_in.shape[1]
    w_pad = jnp.zeros((d, d + 6 * kd + LANES), F32).at[:, :n_in].set(w_in).astype(BF16)
    tile4 = lambda v: jnp.tile(v, kd // HEAD_DIM).reshape(1, kd)
    idx = jnp.arange(kd) // HEAD_DIM
    seg = (idx[:, None] == idx[None, :]).astype(BF16)
    head_spec = lambda nh, w=HEAD_DIM: pl.BlockSpec((1, nh, tm, w), lambda i: (i // nt, 0, i % nt, 0))
    head_shape = lambda nh, w=HEAD_DIM, dt=BF16: jax.ShapeDtypeStruct((bsz, nh, t, w), dt)
    flat_spec = lambda w: pl.BlockSpec((tm, w), lambda i: (i, 0))
    vw2 = 2 * HEAD_DIM
    return pl.pallas_call(
        _nsa_proj_kernel,
        grid=(n // tm,),
        in_specs=[flat_spec(d), _full((1, d)), _full(w_pad.shape), flat_spec(LANES), flat_spec(LANES),
                  _full((1, kd)), _full((1, kd)), _full((1, kd)), _full((kd, kd))],
        out_specs=[head_spec(n_heads), head_spec(N_KV), head_spec(N_KV), head_spec(N_KV), head_spec(N_KV, vw2),
                   head_spec(N_KV), head_spec(N_KV, vw2), flat_spec(LANES)],
        out_shape=[head_shape(n_heads), head_shape(N_KV, dt=F32), head_shape(N_KV, dt=F32),
                   head_shape(N_KV), head_shape(N_KV, vw2), head_shape(N_KV), head_shape(N_KV, vw2),
                   jax.ShapeDtypeStruct((n, LANES), F32)],
        compiler_params=_params(1),
        name="nsa_proj",
    )(h.reshape(n, d), g_mix.reshape(1, d), w_pad, cos_t, sin_t,
      tile4(q_norm), tile4(k_norm[1]), tile4(k_norm[2]), seg)


def _compress_kernel(xk_ref, xv_ref, pe_ref, w1_ref, w2_ref, kn_ref, cos_ref, sin_ref, o_ref):
    nch, dh = o_ref.shape[3], o_ref.shape[4]
    st = CMP_STRIDE
    hidden = w1_ref.shape[2]
    for which, x_ref in enumerate((xk_ref, xv_ref)):
        first = jnp.zeros((nch, hidden), F32)
        second = jnp.zeros((nch, hidden), F32)
        for tp in range(st):
            xt = x_ref[0, 0, pl.ds(tp, nch, stride=st), :]
            first = first + _dot((xt + pe_ref[which, tp:tp + 1, :]).astype(BF16),
                                 w1_ref[which, tp * dh:(tp + 1) * dh, :])
            second = second + _dot((xt + pe_ref[which, st + tp:st + tp + 1, :]).astype(BF16),
                                   w1_ref[which, (st + tp) * dh:(st + tp + 1) * dh, :])
        hid = first + pltpu.roll(second, nch - 1, 0)
        c = _dot(jax.nn.gelu(hid, approximate=True).astype(BF16), w2_ref[which])
        if which == 0:
            half = dh // 2
            cn = _rms(c, kn_ref[...])
            rot = jnp.concatenate([cn[:, half:], cn[:, :half]], axis=1)
            c = cn * cos_ref[0] + rot * sin_ref[0]
        o_ref[which, 0, 0] = c.astype(o_ref.dtype)


def _compress(kc_raw, vc_raw, pe, w1, w2, k_norm0, cos_t, sin_t):
    bsz, g, t, dh = kc_raw.shape
    st = CMP_STRIDE
    nch = t // st
    last = lambda tab: jnp.pad(tab.reshape(bsz, t, LANES)[:, CMP_BLOCK - 1::st, :dh], ((0, 0), (0, 1), (0, 0)))
    src = pl.BlockSpec((1, 1, t, dh), lambda b, gi: (b, gi, 0, 0))
    tab = pl.BlockSpec((1, nch, dh), lambda b, gi: (b, 0, 0))
    return pl.pallas_call(
        _compress_kernel,
        grid=(bsz, g),
        in_specs=[src, src, _full(pe.shape), _full(w1.shape), _full(w2.shape), _full((1, dh)), tab, tab],
        out_specs=pl.BlockSpec((2, 1, 1, nch, dh), lambda b, gi: (0, b, gi, 0, 0)),
        out_shape=jax.ShapeDtypeStruct((2, bsz, g, nch, dh), BF16),
        compiler_params=_params(2),
        name="nsa_compress",
    )(kc_raw, vc_raw, pe, w1.astype(BF16), w2.astype(BF16), k_norm0.reshape(1, dh), last(cos_t), last(sin_t))


def _attend_chunk(q, k, v, bias, m_ref, acc_ref, r_heads, tq_n):
    tk = k.shape[0]
    s = _dot_nt(q, k)
    m_all = m_ref[...]
    acc_all = acc_ref[...]
    m_out, scaled, probs = [], [], []
    for r in range(r_heads):
        rs = slice(r * tq_n, (r + 1) * tq_n)
        sr = s[rs] + bias
        m_old = m_all[rs]
        m_new = jnp.maximum(m_old, jnp.max(sr, axis=-1, keepdims=True))
        m_wide = m_new if tk == m_new.shape[1] else jnp.concatenate([m_new] * (tk // m_new.shape[1]), axis=1)
        probs.append(jnp.exp2(sr - m_wide).astype(BF16))
        m_out.append(m_new)
        scaled.append(jnp.exp2(m_old - m_new) * acc_all[rs])
    m_ref[...] = jnp.concatenate(m_out, axis=0)
    acc_ref[...] = jnp.concatenate(scaled, axis=0) + _dot(jnp.concatenate(probs, axis=0), v)


def _nsa_attn_kernel(q_ref, kc_ref, vc_ref, ks_ref, vs_ref, kw_ref, vw_ref, gate_ref, wsel_ref, e_ref, o_ref,
                     m_ref, acc_ref, og_ref, bias_ref, *, n_sel):
    g = pl.program_id(1)
    qi = pl.program_id(2)
    r_heads, tq_n, dh = q_ref.shape[1], q_ref.shape[2], q_ref.shape[3]
    rows = r_heads * tq_n
    t0 = qi * tq_n
    tk = K_TILE
    q = q_ref[0].reshape(rows, dh)
    head_rows = [slice(r * tq_n, (r + 1) * tq_n) for r in range(r_heads)]

    gates = gate_ref[0]
    gate_lane = lax.broadcasted_iota(I32, gates.shape, 1)

    def gate(r, branch):
        col = (g * r_heads + r) * 3 + branch
        return jnp.sum(jnp.where(gate_lane == col, gates, 0.0), axis=-1, keepdims=True)

    def restart():
        m_ref[...] = jnp.full(m_ref.shape, NEG, F32)
        acc_ref[...] = jnp.zeros(acc_ref.shape, F32)

    def add_branch(branch):
        acc = acc_ref[...]
        o = acc * (1.0 / pltpu.roll(acc, dh, 1))
        og_ref[...] += jnp.concatenate([gate(r, branch) * o[rs, :dh] for r, rs in enumerate(head_rows)], axis=0)

    kc = kc_ref[0, 0]
    nc = kc.shape[0]
    tq_c = t0 + lax.broadcasted_iota(I32, (tq_n, nc), 0)
    mask_c = CMP_STRIDE * lax.broadcasted_iota(I32, (tq_n, nc), 1) + (CMP_BLOCK - 1) <= tq_c
    s_c = _dot_nt(q, kc)
    imp = jnp.zeros((tq_n, nc), F32)
    probs = []
    for rs in head_rows:
        sr = jnp.where(mask_c, s_c[rs], NEG)
        e = jnp.where(mask_c, jnp.exp2(sr - jnp.max(sr, axis=-1, keepdims=True)), 0.0)
        l = jnp.sum(e, axis=-1, keepdims=True)
        p = e * jnp.where(l > 0.0, 1.0 / l, 0.0)
        imp = imp + p
        probs.append(p.astype(BF16))
    o_c = _dot(jnp.concatenate(probs, axis=0), vc_ref[0, 0])
    og_c = jnp.concatenate([gate(r, 0) * o_c[rs] for r, rs in enumerate(head_rows)], axis=0)

    imp_hi = imp.astype(BF16)
    imp_lo = (imp - imp_hi.astype(F32)).astype(BF16)
    p_slc = _dot_nt(wsel_ref[...], imp_hi) + _dot_nt(wsel_ref[...], imp_lo)
    nsb = p_slc.shape[0]
    blk = lax.broadcasted_iota(I32, (nsb, tq_n), 0)
    tq_t = t0 + lax.broadcasted_iota(I32, (nsb, tq_n), 1)
    cur = tq_t // SLC_BLOCK
    forced = (blk == 0) | (blk == cur) | (blk == cur - 1)
    score = jnp.where(forced, jnp.inf, jnp.where(blk * SLC_BLOCK <= tq_t, p_slc, -jnp.inf))
    sub = CONV_SUBLANES
    groups = [slice(g0, g0 + sub) for g0 in range(0, nsb, sub)]
    cnt = [jnp.zeros((sub, tq_n), F32) for _ in groups]
    for i in range(nsb):
        ri = score[i:i + 1, :]
        for gi, rows_g in enumerate(groups):
            sg = score[rows_g]
            if rows_g.start > i:
                beats = ri >= sg
            elif rows_g.stop <= i:
                beats = ri > sg
            else:
                idx = rows_g.start + lax.broadcasted_iota(I32, (sub, tq_n), 0)
                beats = (ri > sg) | ((ri == sg) & (idx > i))
            cnt[gi] = cnt[gi] + jnp.where(beats, 1.0, 0.0)
    dropped = jnp.where(jnp.concatenate(cnt, axis=0) < n_sel, 0.0, 1.0)
    dropped = jnp.concatenate([dropped, jnp.zeros((LANES - nsb, tq_n), F32)], axis=0).T.astype(BF16)

    wq = min(tq_n, WINDOW_Q_TILE)
    kwn = WINDOW + wq
    gates_w = [gate(r, 2) for r in range(r_heads)]
    pieces = [[] for _ in head_rows]
    for h in range(tq_n // wq):
        q_h = jnp.concatenate([q[r * tq_n + h * wq:r * tq_n + (h + 1) * wq] for r in range(r_heads)], axis=0)
        start = pl.multiple_of(jnp.maximum(t0 + h * wq - WINDOW, 0), wq)
        kpos_w = start + lax.broadcasted_iota(I32, (wq, kwn), 1)
        tq_w = t0 + h * wq + lax.broadcasted_iota(I32, (wq, kwn), 0)
        bias_w = jnp.where((kpos_w <= tq_w) & (kpos_w > tq_w - WINDOW), 0.0, NEG)
        s_w = _dot_nt(q_h, kw_ref[0, 0, pl.ds(start, kwn), :])
        probs = []
        for r in range(r_heads):
            sr = s_w[r * wq:(r + 1) * wq] + bias_w
            probs.append(jnp.exp2(sr - jnp.max(sr, axis=-1, keepdims=True)).astype(BF16))
        acc_w = _dot(jnp.concatenate(probs, axis=0), vw_ref[0, 0, pl.ds(start, kwn), :])
        o_w = acc_w * (1.0 / pltpu.roll(acc_w, dh, 1))
        for r in range(r_heads):
            pieces[r].append(gates_w[r][h * wq:(h + 1) * wq] * o_w[r * wq:(r + 1) * wq, :dh])
    og_ref[...] = og_c + jnp.concatenate([piece for per_head in pieces for piece in per_head], axis=0)

    t_keys = e_ref.shape[1]
    kpos = lax.broadcasted_iota(I32, (tq_n, t_keys), 1)
    tq_s = t0 + lax.broadcasted_iota(I32, (tq_n, t_keys), 0)
    bias_all = jnp.where(kpos <= tq_s, _dot(dropped, e_ref[...]), NEG)
    for c in range(t_keys // tk):
        bias_ref[c] = bias_all[:, c * tk:(c + 1) * tk]
    restart()

    def sel_step(c, carry):
        k0 = pl.multiple_of(c * tk, tk)
        _attend_chunk(q, ks_ref[0, 0, pl.ds(k0, tk), :], vs_ref[0, 0, pl.ds(k0, tk), :], bias_ref[c],
                      m_ref, acc_ref, r_heads, tq_n)
        return carry

    lax.fori_loop(0, (t0 + tq_n + tk - 1) // tk, sel_step, 0)
    add_branch(1)

    for r, rs in enumerate(head_rows):
        o_ref[0, :, r * dh:(r + 1) * dh] = og_ref[rs].astype(o_ref.dtype)


def _nsa_attn(q, kc, vc, ks, vs, kw, vw, gate, n_sel):
    bsz, n_heads, t, dh = q.shape
    g = N_KV
    r_heads = n_heads // g
    tq = Q_TILE
    nch = kc.shape[2]
    a = SLC_BLOCK // CMP_STRIDE
    bb = CMP_BLOCK // CMP_STRIDE
    nsb = t // SLC_BLOCK
    j = jnp.arange(nsb)[:, None]
    c = jnp.arange(nch)[None, :]
    wsel = sum(((c == a * j + m + n_) & (c < nch - 1)).astype(F32)
               for m in range(a) for n_ in range(bb)).astype(BF16)
    tk = K_TILE
    key_blk = (jnp.arange(t) // SLC_BLOCK).reshape(1, t)
    e = jnp.where(key_blk == jnp.arange(LANES).reshape(LANES, 1), NEG, 0.0).astype(BF16)
    rows = r_heads * tq
    kv_spec = lambda n_rows, w=dh: pl.BlockSpec((1, 1, n_rows, w), lambda b, gi, qi: (b, gi, 0, 0))
    return pl.pallas_call(
        functools.partial(_nsa_attn_kernel, n_sel=n_sel),
        grid=(bsz, g, t // tq),
        in_specs=[
            pl.BlockSpec((1, r_heads, tq, dh), lambda b, gi, qi: (b, gi, qi, 0)),
            kv_spec(nch), kv_spec(nch), kv_spec(t), kv_spec(t, 2 * dh), kv_spec(t), kv_spec(t, 2 * dh),
            pl.BlockSpec((1, tq, LANES), lambda b, gi, qi: (b, qi, 0)),
            pl.BlockSpec((nsb, nch), lambda b, gi, qi: (0, 0)),
            pl.BlockSpec(e.shape, lambda b, gi, qi: (0, 0)),
        ],
        out_specs=pl.BlockSpec((1, tq, r_heads * dh), lambda b, gi, qi: (b, qi, gi)),
        out_shape=jax.ShapeDtypeStruct((bsz, t, n_heads * dh), BF16),
        scratch_shapes=[pltpu.VMEM((rows, 2 * dh), F32), pltpu.VMEM((rows, 2 * dh), F32),
                        pltpu.VMEM((rows, dh), F32), pltpu.VMEM((t // tk, tq, tk), F32)],
        compiler_params=_params(3),
        name="nsa_attention",
    )(q, kc, vc, ks, vs, kw, vw, gate, wsel, e)


def _out_mix_kernel(o_ref, w_ref, h_ref, gffn_ref, wcat_ref, br_ref, tri_ref,
                    h2_ref, xn_ref, slab_ref, slabt_ref, cnt_ref, carry_ref):
    @pl.when(pl.program_id(0) == 0)
    def _():
        carry_ref[...] = jnp.zeros_like(carry_ref)

    h = h_ref[...] + _dot(o_ref[...], w_ref[...])
    h2_ref[...] = h
    xn, slab = _route(h, gffn_ref[...], wcat_ref, br_ref, tri_ref, carry_ref)
    xn_ref[...] = xn
    slab_ref[...] = slab
    slabt_ref[...] = slab.T[:ROUTE_FIELDS]
    cnt_ref[...] = carry_ref[...]


def _out_mix(o2, w_out, h2, g_ffn, w_rg, b_rg, w_re, b_re):
    n, d = h2.shape
    tm = ROW_TILE
    wcat, br, tri = _router_operands(w_rg, b_rg, w_re, b_re, tm)
    row = pl.BlockSpec((tm, d), lambda i: (i, 0))
    return pl.pallas_call(
        _out_mix_kernel,
        grid=(n // tm,),
        in_specs=[row, _full((d, d)), row, _full((1, d)), _full((d, 2 * LANES)), _full((1, LANES)), _full((tm, tm))],
        out_specs=[row, row, pl.BlockSpec((tm, LANES), lambda i: (i, 0)),
                   pl.BlockSpec((ROUTE_FIELDS, tm), lambda i: (0, i)), _full((1, LANES))],
        out_shape=[jax.ShapeDtypeStruct((n, d), F32), jax.ShapeDtypeStruct((n, d), F32),
                   jax.ShapeDtypeStruct((n, LANES), F32), jax.ShapeDtypeStruct((ROUTE_FIELDS, n), F32),
                   jax.ShapeDtypeStruct((1, LANES), F32)],
        scratch_shapes=[pltpu.VMEM((1, LANES), F32)],
        compiler_params=_params(1),
        name="nsa_out_mix",
    )(o2, w_out.astype(BF16), h2, g_ffn.reshape(1, d), wcat, br, tri)


def _nsa_layer(h, positions, g_mix, w_in, q_norm, k_norm, cmp_pe, cmp_w1, cmp_w2):
    bsz, t, d = h.shape
    cos_t, sin_t = _rope_table(positions)
    q, kc_raw, vc_raw, ks, vs, kw, vw, gate = _nsa_proj(h, g_mix, w_in, cos_t, sin_t, q_norm, k_norm)
    cmp = _compress(kc_raw, vc_raw, cmp_pe, cmp_w1, cmp_w2, k_norm[0], cos_t, sin_t)
    n_sel = min(N_SEL, t // SLC_BLOCK)
    return _nsa_attn(q, cmp[0], cmp[1], ks, vs, kw, vw, gate.reshape(bsz, t, LANES), n_sel)


def kernel(x, p, positions, g_mix, g_ffn, g_ple, conv_w_pw1, conv_b_pw1, conv_w_dw, conv_b_dw, conv_ln_g, conv_ln_b, conv_w_pw2, conv_b_pw2, nsa_w_in, nsa_q_norm, nsa_k_norm, nsa_cmp_pe, nsa_cmp_w1, nsa_cmp_w2, nsa_w_out, moe_w_rg, moe_b_rg, moe_w_re, moe_b_re, moe_w_gate, moe_w_up, moe_w_down, ple_w_proj, ple_w_gate):
    bsz, t, d = x.shape
    n = bsz * t
    depth = p.shape[0]
    h = x
    p3 = p.reshape(depth, n, p.shape[-1])
    for i in range(depth):
        j = i // 2
        route_w = (g_ffn[i], moe_w_rg[i], moe_b_rg[i], moe_w_re[i], moe_b_re[i])
        if i % 2 == 0:
            u = _pw1_glu(h.reshape(n, d), g_mix[i], conv_w_pw1[j], conv_b_pw1[j])
            h, xn, slab, slab_t, counts = _conv_mix(u.reshape(bsz, t, d), h, conv_w_dw[j], conv_b_dw[j],
                                                    conv_ln_g[j], conv_ln_b[j], conv_w_pw2[j], conv_b_pw2[j], *route_w)
        else:
            o = _nsa_layer(h, positions, g_mix[i], nsa_w_in[j], nsa_q_norm[j], nsa_k_norm[j],
                           nsa_cmp_pe[j], nsa_cmp_w1[j], nsa_cmp_w2[j])
            h, xn, slab, slab_t, counts = _out_mix(o.reshape(n, d), nsa_w_out[j], h.reshape(n, d), *route_w)
        h = _moe_ple(h.reshape(n, d), xn, slab, slab_t, counts, i, p3, g_ple[i],
                     moe_w_gate, moe_w_up, moe_w_down, ple_w_gate[i], ple_w_proj[i]).reshape(bsz, t, d)
    return h
```
